```python
import jax, jax.numpy as jnp
from jax import lax
import numpy as np

D_MODEL = 4096
BATCH = 4
SEQ = 4096
DEPTH = 2

CHUNK = 64
N_MIXERS = 2
N_RET_LAYERS = (DEPTH + N_MIXERS - 1) // N_MIXERS
N_LRU_LAYERS = DEPTH // N_MIXERS

RET_HEADS = 16
RET_QK_DIM = D_MODEL // RET_HEADS
RET_V_DIM = 2 * RET_QK_DIM
RET_V_WIDTH = RET_HEADS * RET_V_DIM
ROPE_THETA = 10000.0

LRU_WIDTH = ((4 * D_MODEL // 3) // 128) * 128
LRU_BLOCKS = 16
LRU_BLOCK = LRU_WIDTH // LRU_BLOCKS
CONV_WIDTH = 4
LRU_C = 8.0

D_FF = 4 * D_MODEL
EPS = 1e-6

kernel_name = "retention_rglru_interleaved_trunk"


def rms_norm(x, g):
    xf = x.astype(jnp.float32)
    xf = xf * lax.rsqrt(jnp.mean(xf * xf, axis=-1, keepdims=True) + EPS)
    return (xf * g.astype(jnp.float32)).astype(x.dtype)


def rope(t, positions):
    half = t.shape[-1] // 2
    inv_freq = ROPE_THETA ** (-jnp.arange(half, dtype=jnp.float32) / half)
    ang = positions.astype(jnp.float32)[:, :, None, None] * inv_freq
    cos = jnp.cos(ang).astype(t.dtype)
    sin = jnp.sin(ang).astype(t.dtype)
    t1, t2 = t[..., :half], t[..., half:]
    return jnp.concatenate([t1 * cos - t2 * sin, t2 * cos + t1 * sin], axis=-1)


def retention_mixer(h, positions, w_in, w_out):
    B, S, _ = h.shape
    nc = S // CHUNK
    proj = h @ w_in
    q, k, v, g = jnp.split(proj, [D_MODEL, 2 * D_MODEL, 2 * D_MODEL + RET_V_WIDTH], axis=-1)
    q = rope(q.reshape(B, S, RET_HEADS, RET_QK_DIM), positions)
    k = rope(k.reshape(B, S, RET_HEADS, RET_QK_DIM), positions) * (RET_QK_DIM ** -0.5)
    v = v.reshape(B, S, RET_HEADS, RET_V_DIM)

    def to_chunks(t):
        return t.reshape(B, nc, CHUNK, RET_HEADS, -1).transpose(1, 0, 3, 2, 4)

    log_g = jnp.log1p(-jnp.exp2(-5.0 - jnp.arange(RET_HEADS, dtype=jnp.float32)))
    idx = jnp.arange(CHUNK, dtype=jnp.float32)
    dt = q.dtype
    intra = jnp.exp(log_g[:, None, None] * jnp.abs(idx[:, None] - idx[None, :])).astype(dt)
    q_dec = jnp.exp(log_g[:, None] * (idx + 1.0)).astype(dt)[None, :, :, None]
    k_dec = jnp.exp(log_g[:, None] * (CHUNK - 1.0 - idx)).astype(dt)[None, :, :, None]
    chunk_dec = jnp.exp(log_g * CHUNK).astype(dt)[None, :, None, None]

    def step(state, qkv):
        qc, kc, vc = qkv
        scores = jnp.einsum('bhqd,bhkd->bhqk', qc, kc) * intra[None]
        o = (jnp.einsum('bhqk,bhkv->bhqv', scores, vc)
             + jnp.einsum('bhqd,bhdv->bhqv', qc, state) * q_dec)
        state = state * chunk_dec + jnp.einsum('bhkd,bhkv->bhdv', kc * k_dec, vc)
        return state, o

    state0 = jnp.zeros((B, RET_HEADS, RET_QK_DIM, RET_V_DIM), v.dtype)
    _, o = lax.scan(step, state0, (to_chunks(q), to_chunks(k), to_chunks(v)))
    o = o.transpose(1, 0, 3, 2, 4).reshape(B, S, RET_HEADS, RET_V_DIM)
    of = o.astype(jnp.float32)
    of = of * lax.rsqrt(jnp.mean(of * of, axis=-1, keepdims=True) + EPS)
    o = of.astype(h.dtype).reshape(B, S, RET_V_WIDTH) * jax.nn.silu(g)
    return o @ w_out


def rglru_mixer(h, w_in, conv_w, conv_b, w_rgate, b_rgate, w_igate, b_igate, lam, w_out):
    B, S, _ = h.shape
    proj = h @ w_in
    xb, gb = jnp.split(proj, [LRU_WIDTH], axis=-1)
    xc = lax.conv_general_dilated(
        xb, conv_w.reshape(CONV_WIDTH, 1, LRU_WIDTH).astype(xb.dtype),
        window_strides=(1,), padding=[(CONV_WIDTH - 1, 0)],
        dimension_numbers=('NWC', 'WIO', 'NWC'),
        feature_group_count=LRU_WIDTH) + conv_b
    xblk = xc.reshape(B, S, LRU_BLOCKS, LRU_BLOCK)
    r = jax.nn.sigmoid(jnp.einsum('bsnc,ncd->bsnd', xblk, w_rgate).reshape(B, S, LRU_WIDTH)
                       .astype(jnp.float32) + b_rgate.astype(jnp.float32))
    i = jax.nn.sigmoid(jnp.einsum('bsnc,ncd->bsnd', xblk, w_igate).reshape(B, S, LRU_WIDTH)
                       .astype(jnp.float32) + b_igate.astype(jnp.float32))
    log_a = -LRU_C * r * jax.nn.softplus(-lam.astype(jnp.float32))
    a = jnp.exp(log_a)
    u = jnp.sqrt(-jnp.expm1(2.0 * log_a)) * (i * xc.astype(jnp.float32))

    def step(hc, au):
        a_t, u_t = au
        hc = a_t * hc + u_t
        return hc, hc

    h0 = jnp.zeros((B, LRU_WIDTH), jnp.float32)
    _, hs = lax.scan(step, h0, (a.transpose(1, 0, 2), u.transpose(1, 0, 2)))
    y = hs.transpose(1, 0, 2).astype(h.dtype) * jax.nn.gelu(gb)
    return y @ w_out


def squared_relu_mlp(h, w_up, w_down):
    z = jax.nn.relu(h @ w_up)
    return (z * z) @ w_down


def setup_inputs(seed: int = 0) -> dict:
    key = jax.random.key(seed)
    ks = jax.random.split(key, 20)
    f32 = jnp.float32
    nrm = lambda k, shape, scale: jax.random.normal(k, shape, f32) * scale
    x = jax.random.normal(ks[0], (BATCH, SEQ, D_MODEL), f32)
    positions = jnp.broadcast_to(jnp.arange(SEQ, dtype=jnp.int32)[None, :], (BATCH, SEQ)).astype(jnp.int32)
    norm_mix_g = 1.0 + nrm(ks[1], (DEPTH, D_MODEL), 0.02)
    norm_mlp_g = 1.0 + nrm(ks[2], (DEPTH, D_MODEL), 0.02)
    final_norm_g = 1.0 + nrm(ks[3], (D_MODEL,), 0.02)
    ret_in_width = 2 * D_MODEL + 2 * RET_V_WIDTH
    ret_w_in = nrm(ks[4], (N_RET_LAYERS, D_MODEL, ret_in_width), D_MODEL ** -0.5)
    ret_w_out = nrm(ks[5], (N_RET_LAYERS, RET_V_WIDTH, D_MODEL), RET_V_WIDTH ** -0.5)
    lru_w_in = nrm(ks[6], (N_LRU_LAYERS, D_MODEL, 2 * LRU_WIDTH), D_MODEL ** -0.5)
    lru_conv_w = nrm(ks[7], (N_LRU_LAYERS, CONV_WIDTH, LRU_WIDTH), CONV_WIDTH ** -0.5)
    lru_conv_b = nrm(ks[8], (N_LRU_LAYERS, LRU_WIDTH), 0.01)
    lru_w_rgate = nrm(ks[9], (N_LRU_LAYERS, LRU_BLOCKS, LRU_BLOCK, LRU_BLOCK), LRU_BLOCK ** -0.5)
    lru_b_rgate = nrm(ks[10], (N_LRU_LAYERS, LRU_WIDTH), 0.01)
    lru_w_igate = nrm(ks[11], (N_LRU_LAYERS, LRU_BLOCKS, LRU_BLOCK, LRU_BLOCK), LRU_BLOCK ** -0.5)
    lru_b_igate = nrm(ks[12], (N_LRU_LAYERS, LRU_WIDTH), 0.01)
    a_c = jax.random.uniform(ks[13], (N_LRU_LAYERS, LRU_WIDTH), f32, 0.9, 0.999)
    s = a_c ** (1.0 / LRU_C)
    lru_lambda = jnp.log(s) - jnp.log1p(-s)
    lru_w_out = nrm(ks[14], (N_LRU_LAYERS, LRU_WIDTH, D_MODEL), LRU_WIDTH ** -0.5)
    mlp_w_up = nrm(ks[15], (DEPTH, D_MODEL, D_FF), D_MODEL ** -0.5)
    mlp_w_down = nrm(ks[16], (DEPTH, D_FF, D_MODEL), D_FF ** -0.5)
    return {"x": x, "positions": positions, "norm_mix_g": norm_mix_g, "norm_mlp_g": norm_mlp_g,
            "final_norm_g": final_norm_g, "ret_w_in": ret_w_in, "ret_w_out": ret_w_out,
            "lru_w_in": lru_w_in, "lru_conv_w": lru_conv_w, "lru_conv_b": lru_conv_b,
            "lru_w_rgate": lru_w_rgate, "lru_b_rgate": lru_b_rgate, "lru_w_igate": lru_w_igate,
            "lru_b_igate": lru_b_igate, "lru_lambda": lru_lambda, "lru_w_out": lru_w_out,
            "mlp_w_up": mlp_w_up, "mlp_w_down": mlp_w_down}


def reference(x, positions, norm_mix_g, norm_mlp_g, final_norm_g, ret_w_in, ret_w_out,
              lru_w_in, lru_conv_w, lru_conv_b, lru_w_rgate, lru_b_rgate, lru_w_igate,
              lru_b_igate, lru_lambda, lru_w_out, mlp_w_up, mlp_w_down):
    h = x
    for i in range(DEPTH):
        j = i // N_MIXERS
        y = rms_norm(h, norm_mix_g[i])
        if i % N_MIXERS == 0:
            h = h + retention_mixer(y, positions, ret_w_in[j], ret_w_out[j])
        else:
            h = h + rglru_mixer(y, lru_w_in[j], lru_conv_w[j], lru_conv_b[j], lru_w_rgate[j],
                                lru_b_rgate[j], lru_w_igate[j], lru_b_igate[j], lru_lambda[j],
                                lru_w_out[j])
        y = rms_norm(h, norm_mlp_g[i])
        h = h + squared_relu_mlp(y, mlp_w_up[i], mlp_w_down[i])
    return rms_norm(h, final_norm_g)
```

```python
import functools

import jax
import jax.numpy as jnp
from jax import lax
from jax.experimental import pallas as pl
from jax.experimental.pallas import tpu as pltpu

RET_HEADS = 16
CHUNK = 64
ROPE_THETA = 10000.0
LRU_BLOCKS = 16
CONV_WIDTH = 4
LRU_C = 8.0
EPS = 1e-6

LANES = 128
SUBLANES = 8
VMEM_BYTES_V7X = 64 * 1024 * 1024

MXU_DTYPE = jnp.bfloat16
F32 = jnp.float32

RET_SUB = 256
LRU_SUB = 256


def _round_up(x, m):
    return (x + m - 1) // m * m


def _pick_tile(dim, target, align):
    if dim <= target:
        return dim
    t = target // align * align
    while t >= align:
        if dim % t == 0:
            return t
        t -= align
    return dim


def _nbytes(shape, dtype):
    n = 1
    for s in shape:
        n *= s
    return n * jnp.dtype(dtype).itemsize


def _compiler_params(n_axes, pipelined_bytes, resident_bytes):
    limit = 2 * pipelined_bytes + resident_bytes
    limit = min(max(limit, 16 * 1024 * 1024), VMEM_BYTES_V7X - 4 * 1024 * 1024)
    return pltpu.CompilerParams(
        dimension_semantics=("arbitrary",) * n_axes, vmem_limit_bytes=int(limit))


def _rmsnorm_kernel(x_ref, g_ref, o_ref):
    x = x_ref[...]
    ms = jnp.mean(x * x, axis=-1, keepdims=True)
    o_ref[...] = ((x * lax.rsqrt(ms + EPS)) * g_ref[...]).astype(o_ref.dtype)


def _rmsnorm(x, g, out_dtype):
    m, d = x.shape
    rows = _pick_tile(m, 256, SUBLANES)
    blocks = _nbytes((rows, d), x.dtype) + _nbytes((rows, d), out_dtype)
    return pl.pallas_call(
        _rmsnorm_kernel,
        grid=(m // rows,),
        in_specs=[pl.BlockSpec((rows, d), lambda i: (i, 0)),
                  pl.BlockSpec((1, d), lambda i: (0, 0))],
        out_specs=pl.BlockSpec((rows, d), lambda i: (i, 0)),
        out_shape=jax.ShapeDtypeStruct((m, d), out_dtype),
        compiler_params=_compiler_params(1, blocks, 2 * _nbytes((rows, d), F32)),
        name="rmsnorm",
    )(x, g.reshape(1, d).astype(F32))


def _matmul_kernel(*refs, nk, epilogue):
    if epilogue == "residual":
        x_ref, w_ref, r_ref, o_ref = refs[:4]
    else:
        x_ref, w_ref, o_ref = refs[:3]
        r_ref = None
    acc_ref = refs[-1] if nk > 1 else None

    part = jnp.dot(x_ref[...], w_ref[...], preferred_element_type=F32)

    def finish(acc):
        if epilogue == "relu2":
            z = jnp.maximum(acc, 0.0)
            acc = z * z
        elif epilogue == "residual":
            acc = r_ref[...] + acc
        o_ref[...] = acc.astype(o_ref.dtype)

    if nk == 1:
        finish(part)
        return

    k = pl.program_id(2)

    @pl.when(k == 0)
    def _():
        acc_ref[...] = part

    if nk > 2:
        @pl.when(jnp.logical_and(k > 0, k < nk - 1))
        def _():
            acc_ref[...] += part

    @pl.when(k == nk - 1)
    def _():
        finish(acc_ref[...] + part)


def _matmul(x, w, out_dtype, epilogue="none", res=None, bm=1024, bn=1024, bk=2048):
    m, kdim = x.shape
    n = w.shape[1]
    bm = _pick_tile(m, bm, SUBLANES)
    bn = _pick_tile(n, bn, LANES)
    bk = _pick_tile(kdim, bk, LANES)
    nk = kdim // bk
    in_specs = [pl.BlockSpec((bm, bk), lambda i, j, k: (i, k)),
                pl.BlockSpec((bk, bn), lambda i, j, k: (k, j))]
    args = [x, w]
    blocks = (_nbytes((bm, bk), x.dtype) + _nbytes((bk, bn), w.dtype)
              + _nbytes((bm, bn), out_dtype))
    if epilogue == "residual":
        in_specs.append(pl.BlockSpec((bm, bn), lambda i, j, k: (i, j)))
        args.append(res)
        blocks += _nbytes((bm, bn), res.dtype)
    acc_bytes = _nbytes((bm, bn), F32)
    scratch = [pltpu.VMEM((bm, bn), F32)] if nk > 1 else []
    return pl.pallas_call(
        functools.partial(_matmul_kernel, nk=nk, epilogue=epilogue),
        grid=(m // bm, n // bn, nk),
        in_specs=in_specs,
        out_specs=pl.BlockSpec((bm, bn), lambda i, j, k: (i, j)),
        out_shape=jax.ShapeDtypeStruct((m, n), out_dtype),
        scratch_shapes=scratch,
        compiler_params=_compiler_params(3, blocks, 3 * acc_bytes),
        name="matmul_" + epilogue,
    )(*args)


def _rope_table_kernel(pos_ref, freq_ref, cos_ref, sin_ref):
    ang = pos_ref[...].astype(F32) * freq_ref[...]
    cos_ref[...] = jnp.cos(ang)
    sin_ref[...] = jnp.sin(ang)


def _rope_tables(positions, half):
    m = positions.size
    rows = _pick_tile(m, 1024, SUBLANES)
    inv_freq = ROPE_THETA ** (-jnp.arange(half, dtype=F32) / half)
    out = jax.ShapeDtypeStruct((m, half), F32)
    return pl.pallas_call(
        _rope_table_kernel,
        grid=(m // rows,),
        in_specs=[pl.BlockSpec((rows, 1), lambda i: (i, 0)),
                  pl.BlockSpec((1, half), lambda i: (0, 0))],
        out_specs=[pl.BlockSpec((rows, half), lambda i: (i, 0))] * 2,
        out_shape=[out, out],
        compiler_params=_compiler_params(1, 3 * _nbytes((rows, LANES), F32),
                                         4 * _nbytes((rows, LANES), F32)),
        name="rope_tables",
    )(positions.reshape(m, 1), inv_freq.reshape(1, half))


def _retention_kernel(dec_ref, q_ref, k_ref, v_ref, g_ref, cos_ref, sin_ref, mask_ref,
                      qd_ref, kd_ref, o_ref, state_ref, *, n_sub, scale):
    head = pl.program_id(1)

    @pl.when(pl.program_id(2) == 0)
    def _():
        state_ref[...] = jnp.zeros_like(state_ref)

    dec = dec_ref[head]
    mask = mask_ref[0]
    qd = qd_ref[0]
    kd = kd_ref[0]
    sub = mask.shape[0]
    half = q_ref.shape[-1] // 2

    def rope(t, cos, sin):
        t1, t2 = t[:, :half], t[:, half:]
        return jnp.concatenate([t1 * cos - t2 * sin, t2 * cos + t1 * sin], axis=-1)

    for s in range(n_sub):
        rows = pl.ds(s * sub, sub)
        cos = cos_ref[rows, :]
        sin = sin_ref[rows, :]
        q = rope(q_ref[rows, :], cos, sin)
        k = rope(k_ref[rows, :], cos, sin) * scale
        v = v_ref[rows, :]
        qb = q.astype(MXU_DTYPE)
        kb = k.astype(MXU_DTYPE)
        scores = lax.dot_general(qb, kb, (((1,), (1,)), ((), ())),
                                 preferred_element_type=F32) * mask
        state = state_ref[...]
        o = (jnp.dot(scores.astype(MXU_DTYPE), v, preferred_element_type=F32)
             + jnp.dot(qb, state.astype(MXU_DTYPE), preferred_element_type=F32) * qd)
        kdb = (k * kd).astype(MXU_DTYPE)
        state_ref[...] = state * dec + lax.dot_general(
            kdb, v, (((0,), (0,)), ((), ())), preferred_element_type=F32)
        ms = jnp.mean(o * o, axis=-1, keepdims=True)
        of = o * lax.rsqrt(ms + EPS)
        g = g_ref[rows, :]
        o_ref[rows, :] = (of * (g * jax.nn.sigmoid(g))).astype(o_ref.dtype)


def _retention(qk, v, g, cos, sin, batch, seq):
    m = qk.shape[0]
    heads = RET_HEADS
    dk = qk.shape[1] // (2 * heads)
    dv = v.shape[1] // heads
    sub = _pick_tile(seq, RET_SUB, CHUNK)
    tt = _pick_tile(seq, 1024, sub)
    n_sub = tt // sub
    tpb = seq // tt

    log_g = jnp.log1p(-jnp.exp2(-5.0 - jnp.arange(heads, dtype=F32)))
    idx = jnp.arange(sub, dtype=F32)
    dist = jnp.abs(idx[:, None] - idx[None, :])
    chunk_id = jnp.arange(sub) // CHUNK
    visible = chunk_id[None, :] <= chunk_id[:, None]
    mask = jnp.where(visible[None], jnp.exp(log_g[:, None, None] * dist[None]), 0.0)
    q_dec = jnp.exp(log_g[:, None] * (idx + 1.0))[:, :, None]
    k_dec = jnp.exp(log_g[:, None] * (sub - 1.0 - idx))[:, :, None]
    blk_dec = jnp.exp(log_g * sub)

    row = lambda b, h, t: b * tpb + t
    blocks = (2 * _nbytes((tt, dk), qk.dtype) + _nbytes((tt, dv), v.dtype)
              + _nbytes((tt, dv), g.dtype) + 2 * _nbytes((tt, dk // 2), F32)
              + _nbytes((sub, sub), F32) + 2 * _nbytes((sub, LANES), F32)
              + _nbytes((tt, dv), MXU_DTYPE))
    temps = _nbytes((dk, dv), F32) + 8 * _nbytes((sub, dv), F32)
    return pl.pallas_call(
        functools.partial(_retention_kernel, n_sub=n_sub, scale=dk ** -0.5),
        grid=(batch, heads, tpb),
        in_specs=[
            pl.BlockSpec(memory_space=pltpu.SMEM),
            pl.BlockSpec((tt, dk), lambda b, h, t: (row(b, h, t), h)),
            pl.BlockSpec((tt, dk), lambda b, h, t: (row(b, h, t), heads + h)),
            pl.BlockSpec((tt, dv), lambda b, h, t: (row(b, h, t), h)),
            pl.BlockSpec((tt, dv), lambda b, h, t: (row(b, h, t), h)),
            pl.BlockSpec((tt, dk // 2), lambda b, h, t: (row(b, h, t), 0)),
            pl.BlockSpec((tt, dk // 2), lambda b, h, t: (row(b, h, t), 0)),
            pl.BlockSpec((1, sub, sub), lambda b, h, t: (h, 0, 0)),
            pl.BlockSpec((1, sub, 1), lambda b, h, t: (h, 0, 0)),
            pl.BlockSpec((1, sub, 1), lambda b, h, t: (h, 0, 0)),
        ],
        out_specs=pl.BlockSpec((tt, dv), lambda b, h, t: (row(b, h, t), h)),
        out_shape=jax.ShapeDtypeStruct((m, heads * dv), MXU_DTYPE),
        scratch_shapes=[pltpu.VMEM((dk, dv), F32)],
        compiler_params=_compiler_params(3, blocks, temps),
        name="retention",
    )(blk_dec, qk, qk, v, g, cos, sin, mask, q_dec, k_dec)


_P_CONV_B, _P_BIAS_R, _P_BIAS_I, _P_LAMBDA = CONV_WIDTH, CONV_WIDTH + 1, CONV_WIDTH + 2, CONV_WIDTH + 3


def _gelu_tanh(x):
    c = 0.7978845608028654
    return x * (0.5 * (1.0 + jnp.tanh(c * (x + 0.044715 * (x * x * x)))))


def _lru_kernel(x_ref, gb_ref, w_ref, p_ref, o_ref, xe_ref, h_ref, *, sub, n_sub):
    tt, pw = x_ref.shape
    tail = SUBLANES

    @pl.when(pl.program_id(2) == 0)
    def _():
        h_ref[...] = jnp.zeros_like(h_ref)
        xe_ref[0:tail, :] = jnp.zeros((tail, pw), F32)

    xe_ref[tail:tail + tt, :] = x_ref[...]

    p = p_ref[...]
    conv_w = [p[k:k + 1, :] for k in range(CONV_WIDTH)]
    conv_b = p[_P_CONV_B:_P_CONV_B + 1, :]
    bias_r = p[_P_BIAS_R:_P_BIAS_R + 1, :]
    bias_i = p[_P_BIAS_I:_P_BIAS_I + 1, :]
    neg_lam = -p[_P_LAMBDA:_P_LAMBDA + 1, :]
    softplus = jnp.maximum(neg_lam, 0.0) + jnp.log1p(jnp.exp(-jnp.abs(neg_lam)))
    w = w_ref[0]
    n_tiles = sub // SUBLANES
    carry = h_ref[...]

    for c in range(n_sub):
        base = tail + c * sub
        xc = conv_b
        for k in range(CONV_WIDTH):
            xc = xc + conv_w[k] * xe_ref[pl.ds(base - (CONV_WIDTH - 1) + k, sub), :]
        gates = jnp.dot(xc.astype(MXU_DTYPE), w, preferred_element_type=F32)
        r = jax.nn.sigmoid(gates[:, :pw] + bias_r)
        i = jax.nn.sigmoid(gates[:, pw:] + bias_i)
        log_a = (-LRU_C * r) * softplus
        a = jnp.exp(log_a)
        u = jnp.sqrt(1.0 - a * a) * (i * xc)

        a3 = a.reshape(n_tiles, SUBLANES, pw)
        u3 = u.reshape(n_tiles, SUBLANES, pw)
        row = lax.broadcasted_iota(jnp.int32, a3.shape, 1)
        d = 1
        while d < SUBLANES:
            keep = row >= d
            a_prev = jnp.where(keep, pltpu.roll(a3, d, 1), 1.0)
            u_prev = jnp.where(keep, pltpu.roll(u3, d, 1), 0.0)
            u3 = u3 + a3 * u_prev
            a3 = a3 * a_prev
            d *= 2

        a_tot = jnp.broadcast_to(a3[:, SUBLANES - 1:SUBLANES, :], a3.shape)
        u_tot = jnp.broadcast_to(u3[:, SUBLANES - 1:SUBLANES, :], u3.shape)
        h_in = []
        for t in range(n_tiles):
            h_in.append(carry)
            carry = u_tot[t] + a_tot[t] * carry
        hs = (u3 + a3 * jnp.stack(h_in)).reshape(sub, pw)

        gb = gb_ref[pl.ds(c * sub, sub), :]
        o_ref[pl.ds(c * sub, sub), :] = (hs * _gelu_tanh(gb)).astype(o_ref.dtype)

    h_ref[...] = carry
    xe_ref[0:tail, :] = xe_ref[tt:tt + tail, :]


def _lru(proj, w_gates, params, batch, seq):
    m = proj.shape[0]
    nb = LRU_BLOCKS
    pw = proj.shape[1] // (2 * nb)
    sub = _pick_tile(seq, LRU_SUB, SUBLANES)
    tt = _pick_tile(seq, 1024, sub)
    tpb = seq // tt
    row = lambda n, b, t: b * tpb + t
    blocks = (2 * _nbytes((tt, pw), proj.dtype) + _nbytes((pw, 2 * pw), w_gates.dtype)
              + _nbytes((SUBLANES, pw), F32) + _nbytes((tt, pw), MXU_DTYPE))
    temps = _nbytes((tt + SUBLANES, pw), F32) + 16 * _nbytes((sub, 2 * pw), F32)
    return pl.pallas_call(
        functools.partial(_lru_kernel, sub=sub, n_sub=tt // sub),
        grid=(nb, batch, tpb),
        in_specs=[
            pl.BlockSpec((tt, pw), lambda n, b, t: (row(n, b, t), n)),
            pl.BlockSpec((tt, pw), lambda n, b, t: (row(n, b, t), nb + n)),
            pl.BlockSpec((1, pw, 2 * pw), lambda n, b, t: (n, 0, 0)),
            pl.BlockSpec((SUBLANES, pw), lambda n, b, t: (0, n)),
        ],
        out_specs=pl.BlockSpec((tt, pw), lambda n, b, t: (row(n, b, t), n)),
        out_shape=jax.ShapeDtypeStruct((m, nb * pw), MXU_DTYPE),
        scratch_shapes=[pltpu.VMEM((tt + SUBLANES, pw), F32),
                        pltpu.VMEM((SUBLANES, pw), F32)],
        compiler_params=_compiler_params(3, blocks, temps),
        name="rglru",
    )(proj, proj, w_gates, params)


def _pad_blocks(a, axis, width):
    blk = a.shape[axis] // LRU_BLOCKS
    a = a.reshape(a.shape[:axis] + (LRU_BLOCKS, blk) + a.shape[axis + 1:])
    pad = [(0, 0)] * a.ndim
    pad[axis + 1] = (0, width - blk)
    a = jnp.pad(a, pad)
    return a.reshape(a.shape[:axis] + (LRU_BLOCKS * width,) + a.shape[axis + 2:])


def _mlp(h, g, w_up, w_down):
    y = _rmsnorm(h, g, MXU_DTYPE)
    z = _matmul(y, w_up.astype(MXU_DTYPE), MXU_DTYPE, epilogue="relu2")
    return _matmul(z, w_down.astype(MXU_DTYPE), F32, epilogue="residual", res=h)


def _retention_layer(h, g, positions, w_in, w_out, batch, seq):
    d = h.shape[1]
    v_width = (w_in.shape[1] - 2 * d) // 2
    y = _rmsnorm(h, g, MXU_DTYPE)
    w_qk = w_in[:, :2 * d].astype(MXU_DTYPE)
    w_v = w_in[:, 2 * d:2 * d + v_width].astype(MXU_DTYPE)
    w_g = w_in[:, 2 * d + v_width:].astype(MXU_DTYPE)
    qk = _matmul(y, w_qk, F32)
    v = _matmul(y, w_v, MXU_DTYPE)
    gate = _matmul(y, w_g, F32)
    cos, sin = _rope_tables(positions, d // RET_HEADS // 2)
    o = _retention(qk, v, gate, cos, sin, batch, seq)
    return _matmul(o, w_out.astype(MXU_DTYPE), F32, epilogue="residual", res=h)


def _lru_layer(h, g, w_in, conv_w, conv_b, w_rgate, b_rgate, w_igate, b_igate, lam, w_out,
               batch, seq):
    width = w_out.shape[0]
    pw = _round_up(width // LRU_BLOCKS, LANES)
    y = _rmsnorm(h, g, MXU_DTYPE)
    w_in_p = jnp.concatenate(
        [_pad_blocks(w_in[:, :width], 1, pw), _pad_blocks(w_in[:, width:], 1, pw)],
        axis=1).astype(MXU_DTYPE)
    blk = width // LRU_BLOCKS
    pad_sq = [(0, 0), (0, pw - blk), (0, pw - blk)]
    w_gates = jnp.concatenate([jnp.pad(w_rgate, pad_sq), jnp.pad(w_igate, pad_sq)],
                              axis=2).astype(MXU_DTYPE)
    params = jnp.concatenate(
        [_pad_blocks(conv_w, 1, pw)]
        + [_pad_blocks(v.reshape(1, width), 1, pw) for v in (conv_b, b_rgate, b_igate, lam)],
        axis=0).astype(F32)
    w_out_p = _pad_blocks(w_out, 0, pw).astype(MXU_DTYPE)
    proj = _matmul(y, w_in_p, F32)
    mixed = _lru(proj, w_gates, params, batch, seq)
    return _matmul(mixed, w_out_p, F32, epilogue="residual", res=h)


def kernel(x, positions, norm_mix_g, norm_mlp_g, final_norm_g, ret_w_in, ret_w_out,
           lru_w_in, lru_conv_w, lru_conv_b, lru_w_rgate, lru_b_rgate, lru_w_igate,
           lru_b_igate, lru_lambda, lru_w_out, mlp_w_up, mlp_w_down):
    batch, seq, d = x.shape
    h = x.reshape(batch * seq, d)
    depth = norm_mix_g.shape[0]
    n_mixers = 2
    for i in range(depth):
        j = i // n_mixers
        if i % n_mixers == 0:
            h = _retention_layer(h, norm_mix_g[i], positions, ret_w_in[j], ret_w_out[j],
                                 batch, seq)
        else:
            h = _lru_layer(h, norm_mix_g[i], lru_w_in[j], lru_conv_w[j], lru_conv_b[j],
                           lru_w_rgate[j], lru_b_rgate[j], lru_w_igate[j], lru_b_igate[j],
                           lru_lambda[j], lru_w_out[j], batch, seq)
        h = _mlp(h, norm_mlp_g[i], mlp_w_up[i], mlp_w_down[i])
    return _rmsnorm(h, final_norm_g, x.dtype).reshape(batch, seq, d)
```

```python
import functools

import jax
import jax.numpy as jnp
from jax import lax
from jax.experimental import pallas as pl
from jax.experimental.pallas import tpu as pltpu

RET_HEADS = 16
CHUNK = 64
ROPE_THETA = 10000.0
LRU_BLOCKS = 16
CONV_WIDTH = 4
LRU_C = 8.0
EPS = 1e-6

LANES = 128
SUBLANES = 8
VMEM_BYTES_V7X = 64 * 1024 * 1024

MXU_DTYPE = jnp.bfloat16
F32 = jnp.float32

RET_SUB = 256
LRU_SUB = 256


def _round_up(x, m):
    return (x + m - 1) // m * m


def _pick_tile(dim, target, align):
    if dim <= target:
        return dim
    t = target // align * align
    while t >= align:
        if dim % t == 0:
            return t
        t -= align
    return dim


def _nbytes(shape, dtype):
    n = 1
    for s in shape:
        n *= s
    return n * jnp.dtype(dtype).itemsize


def _compiler_params(n_axes, pipelined_bytes, resident_bytes):
    limit = 2 * pipelined_bytes + resident_bytes
    limit = min(max(limit, 16 * 1024 * 1024), VMEM_BYTES_V7X - 4 * 1024 * 1024)
    return pltpu.CompilerParams(
        dimension_semantics=("arbitrary",) * n_axes, vmem_limit_bytes=int(limit))


def _rmsnorm_kernel(x_ref, g_ref, o_ref):
    x = x_ref[...]
    ms = jnp.mean(x * x, axis=-1, keepdims=True)
    o_ref[...] = ((x * lax.rsqrt(ms + EPS)) * g_ref[...]).astype(o_ref.dtype)


def _rmsnorm(x, g, out_dtype):
    m, d = x.shape
    rows = _pick_tile(m, 256, SUBLANES)
    blocks = _nbytes((rows, d), x.dtype) + _nbytes((rows, d), out_dtype)
    return pl.pallas_call(
        _rmsnorm_kernel,
        grid=(m // rows,),
        in_specs=[pl.BlockSpec((rows, d), lambda i: (i, 0)),
                  pl.BlockSpec((1, d), lambda i: (0, 0))],
        out_specs=pl.BlockSpec((rows, d), lambda i: (i, 0)),
        out_shape=jax.ShapeDtypeStruct((m, d), out_dtype),
        compiler_params=_compiler_params(1, blocks, 2 * _nbytes((rows, d), F32)),
        name="rmsnorm",
    )(x, g.reshape(1, d).astype(F32))


def _matmul_kernel(*refs, nk, epilogue):
    if epilogue == "residual":
        x_ref, w_ref, r_ref, o_ref = refs
    else:
        x_ref, w_ref, o_ref = refs
        r_ref = None

    def product():
        return jnp.dot(x_ref[...], w_ref[...], preferred_element_type=F32)

    def first():
        acc = product()
        if epilogue == "relu2":
            z = jnp.maximum(acc, 0.0)
            acc = z * z
        elif epilogue == "residual":
            acc = r_ref[...] + acc
        o_ref[...] = acc.astype(o_ref.dtype)

    if nk == 1:
        first()
        return

    k = pl.program_id(2)
    pl.when(k == 0)(first)

    @pl.when(k > 0)
    def _():
        o_ref[...] += product()


def _matmul(x, w, out_dtype, epilogue="none", res=None, cols=None, bm=1024, bn=1024, bk=4096):
    m, kdim = x.shape
    c0, c1 = cols if cols is not None else (0, w.shape[1])
    n = c1 - c0
    bm = _pick_tile(m, bm, SUBLANES)
    bn = _pick_tile(n, bn, LANES)
    bk = _pick_tile(kdim, bk, LANES)
    nk = kdim // bk
    assert nk == 1 or (out_dtype == F32 and epilogue != "relu2")
    assert c0 % bn == 0
    j0 = c0 // bn
    in_specs = [pl.BlockSpec((bm, bk), lambda i, j, k: (i, k)),
                pl.BlockSpec((bk, bn), lambda i, j, k: (k, j0 + j))]
    args = [x, w]
    blocks = (_nbytes((bm, bk), x.dtype) + _nbytes((bk, bn), w.dtype)
              + _nbytes((bm, bn), out_dtype))
    if epilogue == "residual":
        in_specs.append(pl.BlockSpec((bm, bn), lambda i, j, k: (i, j)))
        args.append(res)
        blocks += _nbytes((bm, bn), res.dtype)
    return pl.pallas_call(
        functools.partial(_matmul_kernel, nk=nk, epilogue=epilogue),
        grid=(m // bm, n // bn, nk),
        in_specs=in_specs,
        out_specs=pl.BlockSpec((bm, bn), lambda i, j, k: (i, j)),
        out_shape=jax.ShapeDtypeStruct((m, n), out_dtype),
        compiler_params=_compiler_params(3, blocks, 2 * _nbytes((bm, bn), F32)),
        name="matmul_" + epilogue,
    )(*args)


def _rope_table_kernel(pos_ref, freq_ref, cos_ref, sin_ref):
    ang = pos_ref[...].astype(F32) * freq_ref[...]
    cos_ref[...] = jnp.cos(ang)
    sin_ref[...] = jnp.sin(ang)


def _rope_tables(positions, half):
    m = positions.size
    rows = _pick_tile(m, 1024, SUBLANES)
    inv_freq = ROPE_THETA ** (-jnp.arange(half, dtype=F32) / half)
    out = jax.ShapeDtypeStruct((m, half), F32)
    return pl.pallas_call(
        _rope_table_kernel,
        grid=(m // rows,),
        in_specs=[pl.BlockSpec((rows, 1), lambda i: (i, 0)),
                  pl.BlockSpec((1, half), lambda i: (0, 0))],
        out_specs=[pl.BlockSpec((rows, half), lambda i: (i, 0))] * 2,
        out_shape=[out, out],
        compiler_params=_compiler_params(1, 3 * _nbytes((rows, LANES), F32),
                                         4 * _nbytes((rows, LANES), F32)),
        name="rope_tables",
    )(positions.reshape(m, 1), inv_freq.reshape(1, half))


def _retention_kernel(dec_ref, q_ref, k_ref, v_ref, g_ref, cos_ref, sin_ref, mask_ref,
                      qd_ref, kd_ref, o_ref, state_ref, *, n_sub, scale):
    head = pl.program_id(1)

    @pl.when(pl.program_id(2) == 0)
    def _():
        state_ref[...] = jnp.zeros_like(state_ref)

    dec = dec_ref[head]
    mask = mask_ref[0]
    qd = qd_ref[0]
    kd = kd_ref[0]
    sub = mask.shape[0]
    half = q_ref.shape[-1] // 2

    def rope(t, cos, sin):
        t1, t2 = t[:, :half], t[:, half:]
        return jnp.concatenate([t1 * cos - t2 * sin, t2 * cos + t1 * sin], axis=-1)

    for s in range(n_sub):
        rows = pl.ds(s * sub, sub)
        cos = cos_ref[rows, :]
        sin = sin_ref[rows, :]
        q = rope(q_ref[rows, :], cos, sin)
        k = rope(k_ref[rows, :], cos, sin) * scale
        v = v_ref[rows, :]
        qb = q.astype(MXU_DTYPE)
        kb = k.astype(MXU_DTYPE)
        scores = lax.dot_general(qb, kb, (((1,), (1,)), ((), ())),
                                 preferred_element_type=F32) * mask
        state = state_ref[...]
        o = (jnp.dot(scores.astype(MXU_DTYPE), v, preferred_element_type=F32)
             + jnp.dot(qb, state.astype(MXU_DTYPE), preferred_element_type=F32) * qd)
        kdb = (k * kd).astype(MXU_DTYPE)
        state_ref[...] = state * dec + lax.dot_general(
            kdb, v, (((0,), (0,)), ((), ())), preferred_element_type=F32)
        ms = jnp.mean(o * o, axis=-1, keepdims=True)
        of = o * lax.rsqrt(ms + EPS)
        g = g_ref[rows, :]
        o_ref[rows, :] = (of * (g * jax.nn.sigmoid(g))).astype(o_ref.dtype)


def _retention(qk, v, g, cos, sin, batch, seq):
    m = qk.shape[0]
    heads = RET_HEADS
    dk = qk.shape[1] // (2 * heads)
    dv = v.shape[1] // heads
    sub = _pick_tile(seq, RET_SUB, CHUNK)
    tt = _pick_tile(seq, 1024, sub)
    n_sub = tt // sub
    tpb = seq // tt

    log_g = jnp.log1p(-jnp.exp2(-5.0 - jnp.arange(heads, dtype=F32)))
    idx = jnp.arange(sub, dtype=F32)
    dist = jnp.abs(idx[:, None] - idx[None, :])
    chunk_id = jnp.arange(sub) // CHUNK
    visible = chunk_id[None, :] <= chunk_id[:, None]
    mask = jnp.where(visible[None], jnp.exp(log_g[:, None, None] * dist[None]), 0.0)
    q_dec = jnp.exp(log_g[:, None] * (idx + 1.0))[:, :, None]
    k_dec = jnp.exp(log_g[:, None] * (sub - 1.0 - idx))[:, :, None]
    blk_dec = jnp.exp(log_g * sub)

    row = lambda b, h, t: b * tpb + t
    blocks = (2 * _nbytes((tt, dk), qk.dtype) + _nbytes((tt, dv), v.dtype)
              + _nbytes((tt, dv), g.dtype) + 2 * _nbytes((tt, dk // 2), F32)
              + _nbytes((sub, sub), F32) + 2 * _nbytes((sub, LANES), F32)
              + _nbytes((tt, dv), MXU_DTYPE))
    temps = _nbytes((dk, dv), F32) + 8 * _nbytes((sub, dv), F32)
    return pl.pallas_call(
        functools.partial(_retention_kernel, n_sub=n_sub, scale=dk ** -0.5),
        grid=(batch, heads, tpb),
        in_specs=[
            pl.BlockSpec(memory_space=pltpu.SMEM),
            pl.BlockSpec((tt, dk), lambda b, h, t: (row(b, h, t), h)),
            pl.BlockSpec((tt, dk), lambda b, h, t: (row(b, h, t), heads + h)),
            pl.BlockSpec((tt, dv), lambda b, h, t: (row(b, h, t), h)),
            pl.BlockSpec((tt, dv), lambda b, h, t: (row(b, h, t), h)),
            pl.BlockSpec((tt, dk // 2), lambda b, h, t: (row(b, h, t), 0)),
            pl.BlockSpec((tt, dk // 2), lambda b, h, t: (row(b, h, t), 0)),
            pl.BlockSpec((1, sub, sub), lambda b, h, t: (h, 0, 0)),
            pl.BlockSpec((1, sub, 1), lambda b, h, t: (h, 0, 0)),
            pl.BlockSpec((1, sub, 1), lambda b, h, t: (h, 0, 0)),
        ],
        out_specs=pl.BlockSpec((tt, dv), lambda b, h, t: (row(b, h, t), h)),
        out_shape=jax.ShapeDtypeStruct((m, heads * dv), MXU_DTYPE),
        scratch_shapes=[pltpu.VMEM((dk, dv), F32)],
        compiler_params=_compiler_params(3, blocks, temps),
        name="retention",
    )(blk_dec, qk, qk, v, g, cos, sin, mask, q_dec, k_dec)


_P_CONV_B, _P_BIAS_R, _P_BIAS_I, _P_LAMBDA = CONV_WIDTH, CONV_WIDTH + 1, CONV_WIDTH + 2, CONV_WIDTH + 3


def _gelu_tanh(x):
    c = 0.7978845608028654
    return x * (0.5 * (1.0 + jnp.tanh(c * (x + 0.044715 * (x * x * x)))))


def _lru_kernel(x_ref, gb_ref, w_ref, p_ref, o_ref, xe_ref, h_ref, *, sub, n_sub):
    tt, pw = x_ref.shape
    tail = SUBLANES

    @pl.when(pl.program_id(2) == 0)
    def _():
        h_ref[...] = jnp.zeros_like(h_ref)
        xe_ref[0:tail, :] = jnp.zeros((tail, pw), F32)

    xe_ref[tail:tail + tt, :] = x_ref[...]

    p = p_ref[...]
    conv_w = [p[k:k + 1, :] for k in range(CONV_WIDTH)]
    conv_b = p[_P_CONV_B:_P_CONV_B + 1, :]
    bias_r = p[_P_BIAS_R:_P_BIAS_R + 1, :]
    bias_i = p[_P_BIAS_I:_P_BIAS_I + 1, :]
    neg_lam = -p[_P_LAMBDA:_P_LAMBDA + 1, :]
    softplus = jnp.maximum(neg_lam, 0.0) + jnp.log1p(jnp.exp(-jnp.abs(neg_lam)))
    w = w_ref[0]
    n_tiles = sub // SUBLANES
    carry = h_ref[...]

    for c in range(n_sub):
        base = tail + c * sub
        xc = conv_b
        for k in range(CONV_WIDTH):
            xc = xc + conv_w[k] * xe_ref[pl.ds(base - (CONV_WIDTH - 1) + k, sub), :]
        gates = jnp.dot(xc.astype(MXU_DTYPE), w, preferred_element_type=F32)
        r = jax.nn.sigmoid(gates[:, :pw] + bias_r)
        i = jax.nn.sigmoid(gates[:, pw:] + bias_i)
        log_a = (-LRU_C * r) * softplus
        a = jnp.exp(log_a)
        u = jnp.sqrt(1.0 - a * a) * (i * xc)

        a3 = a.reshape(n_tiles, SUBLANES, pw)
        u3 = u.reshape(n_tiles, SUBLANES, pw)
        row = lax.broadcasted_iota(jnp.int32, a3.shape, 1)
        d = 1
        while d < SUBLANES:
            keep = row >= d
            a_prev = jnp.where(keep, pltpu.roll(a3, d, 1), 1.0)
            u_prev = jnp.where(keep, pltpu.roll(u3, d, 1), 0.0)
            u3 = u3 + a3 * u_prev
            a3 = a3 * a_prev
            d *= 2

        a_tot = jnp.broadcast_to(a3[:, SUBLANES - 1:SUBLANES, :], a3.shape)
        u_tot = jnp.broadcast_to(u3[:, SUBLANES - 1:SUBLANES, :], u3.shape)
        h_in = []
        for t in range(n_tiles):
            h_in.append(carry)
            carry = u_tot[t] + a_tot[t] * carry
        hs = (u3 + a3 * jnp.stack(h_in)).reshape(sub, pw)

        gb = gb_ref[pl.ds(c * sub, sub), :]
        o_ref[pl.ds(c * sub, sub), :] = (hs * _gelu_tanh(gb)).astype(o_ref.dtype)

    h_ref[...] = carry
    xe_ref[0:tail, :] = xe_ref[tt:tt + tail, :]


def _lru(proj, w_gates, params, batch, seq):
    m = proj.shape[0]
    nb = LRU_BLOCKS
    pw = proj.shape[1] // (2 * nb)
    sub = _pick_tile(seq, LRU_SUB, SUBLANES)
    tt = _pick_tile(seq, 1024, sub)
    tpb = seq // tt
    row = lambda n, b, t: b * tpb + t
    blocks = (2 * _nbytes((tt, pw), proj.dtype) + _nbytes((pw, 2 * pw), w_gates.dtype)
              + _nbytes((SUBLANES, pw), F32) + _nbytes((tt, pw), MXU_DTYPE))
    temps = _nbytes((tt + SUBLANES, pw), F32) + 16 * _nbytes((sub, 2 * pw), F32)
    return pl.pallas_call(
        functools.partial(_lru_kernel, sub=sub, n_sub=tt // sub),
        grid=(nb, batch, tpb),
        in_specs=[
            pl.BlockSpec((tt, pw), lambda n, b, t: (row(n, b, t), n)),
            pl.BlockSpec((tt, pw), lambda n, b, t: (row(n, b, t), nb + n)),
            pl.BlockSpec((1, pw, 2 * pw), lambda n, b, t: (n, 0, 0)),
            pl.BlockSpec((SUBLANES, pw), lambda n, b, t: (0, n)),
        ],
        out_specs=pl.BlockSpec((tt, pw), lambda n, b, t: (row(n, b, t), n)),
        out_shape=jax.ShapeDtypeStruct((m, nb * pw), MXU_DTYPE),
        scratch_shapes=[pltpu.VMEM((tt + SUBLANES, pw), F32),
                        pltpu.VMEM((SUBLANES, pw), F32)],
        compiler_params=_compiler_params(3, blocks, temps),
        name="rglru",
    )(proj, proj, w_gates, params)


def _pad_blocks(a, axis, width):
    blk = a.shape[axis] // LRU_BLOCKS
    a = a.reshape(a.shape[:axis] + (LRU_BLOCKS, blk) + a.shape[axis + 1:])
    pad = [(0, 0)] * a.ndim
    pad[axis + 1] = (0, width - blk)
    a = jnp.pad(a, pad)
    return a.reshape(a.shape[:axis] + (LRU_BLOCKS * width,) + a.shape[axis + 2:])


def _mlp(h, g, w_up, w_down):
    y = _rmsnorm(h, g, MXU_DTYPE)
    z = _matmul(y, w_up.astype(MXU_DTYPE), MXU_DTYPE, epilogue="relu2")
    return _matmul(z, w_down.astype(MXU_DTYPE), F32, epilogue="residual", res=h, bk=2048)


def _retention_layer(h, g, positions, w_in, w_out, batch, seq):
    d = h.shape[1]
    v_width = (w_in.shape[1] - 2 * d) // 2
    y = _rmsnorm(h, g, MXU_DTYPE)
    w = w_in.astype(MXU_DTYPE)
    qk = _matmul(y, w, F32, cols=(0, 2 * d))
    v = _matmul(y, w, MXU_DTYPE, cols=(2 * d, 2 * d + v_width))
    gate = _matmul(y, w, F32, cols=(2 * d + v_width, 2 * d + 2 * v_width))
    cos, sin = _rope_tables(positions, d // RET_HEADS // 2)
    o = _retention(qk, v, gate, cos, sin, batch, seq)
    return _matmul(o, w_out.astype(MXU_DTYPE), F32, epilogue="residual", res=h, bk=2048)


def _lru_layer(h, g, w_in, conv_w, conv_b, w_rgate, b_rgate, w_igate, b_igate, lam, w_out,
               batch, seq):
    width = w_out.shape[0]
    pw = _round_up(width // LRU_BLOCKS, LANES)
    y = _rmsnorm(h, g, MXU_DTYPE)
    w_in_b = w_in.astype(MXU_DTYPE)
    w_in_p = jnp.concatenate(
        [_pad_blocks(w_in_b[:, :width], 1, pw), _pad_blocks(w_in_b[:, width:], 1, pw)], axis=1)
    blk = width // LRU_BLOCKS
    pad_sq = [(0, 0), (0, pw - blk), (0, pw - blk)]
    w_gates = jnp.concatenate([jnp.pad(w_rgate, pad_sq), jnp.pad(w_igate, pad_sq)],
                              axis=2).astype(MXU_DTYPE)
    params = jnp.concatenate(
        [_pad_blocks(conv_w, 1, pw)]
        + [_pad_blocks(v.reshape(1, width), 1, pw) for v in (conv_b, b_rgate, b_igate, lam)],
        axis=0).astype(F32)
    w_out_p = _pad_blocks(w_out.astype(MXU_DTYPE), 0, pw)
    proj = _matmul(y, w_in_p, F32)
    mixed = _lru(proj, w_gates, params, batch, seq)
    return _matmul(mixed, w_out_p, F32, epilogue="residual", res=h, bk=2048)


def kernel(x, positions, norm_mix_g, norm_mlp_g, final_norm_g, ret_w_in, ret_w_out,
           lru_w_in, lru_conv_w, lru_conv_b, lru_w_rgate, lru_b_rgate, lru_w_igate,
           lru_b_igate, lru_lambda, lru_w_out, mlp_w_up, mlp_w_down):
    batch, seq, d = x.shape
    h = x.reshape(batch * seq, d)
    depth = norm_mix_g.shape[0]
    n_mixers = 2
    for i in range(depth):
        j = i // n_mixers
        if i % n_mixers == 0:
            h = _retention_layer(h, norm_mix_g[i], positions, ret_w_in[j], ret_w_out[j],
                                 batch, seq)
        else:
            h = _lru_layer(h, norm_mix_g[i], lru_w_in[j], lru_conv_w[j], lru_conv_b[j],
                           lru_w_rgate[j], lru_b_rgate[j], lru_w_igate[j], lru_b_igate[j],
                           lru_lambda[j], lru_w_out[j], batch, seq)
        h = _mlp(h, norm_mlp_g[i], mlp_w_up[i], mlp_w_down[i])
    return _rmsnorm(h, final_norm_g, x.dtype).reshape(batch, seq, d)
```

```python
import functools
import math

import jax
import jax.numpy as jnp
from jax import lax
from jax.experimental import pallas as pl
from jax.experimental.pallas import tpu as pltpu

RET_HEADS = 16
CHUNK = 64
ROPE_THETA = 10000.0
LRU_BLOCKS = 16
CONV_WIDTH = 4
LRU_C = 8.0
EPS = 1e-6

LANES = 128
SUBLANES = 8
MXU_WIDTH = 256
VMEM_BYTES_V7X = 64 * 1024 * 1024

MXU_DTYPE = jnp.bfloat16
F32 = jnp.float32

RET_SUB = 256
LRU_ROWS = 256
SQRT_FLOOR = 1e-30


def _round_up(x, m):
    return (x + m - 1) // m * m


def _pick_tile(dim, target, align):
    if dim <= target:
        return dim
    t = target // align * align
    while t >= align:
        if dim % t == 0:
            return t
        t -= align
    return dim


def _nbytes(shape, dtype):
    n = 1
    for s in shape:
        n *= s
    return n * jnp.dtype(dtype).itemsize


def _compiler_params(n_axes, pipelined_bytes, resident_bytes):
    limit = 2 * pipelined_bytes + resident_bytes
    limit = min(max(limit, 16 * 1024 * 1024), VMEM_BYTES_V7X - 4 * 1024 * 1024)
    return pltpu.CompilerParams(
        dimension_semantics=("arbitrary",) * n_axes, vmem_limit_bytes=int(limit))


def _rmsnorm_kernel(x_ref, g_ref, o_ref):
    x = x_ref[...]
    ms = jnp.mean(x * x, axis=-1, keepdims=True)
    o_ref[...] = ((x * lax.rsqrt(ms + EPS)) * g_ref[...]).astype(o_ref.dtype)


def _rmsnorm(x, g, out_dtype):
    m, d = x.shape
    rows = _pick_tile(m, 256, SUBLANES)
    blocks = _nbytes((rows, d), x.dtype) + _nbytes((rows, d), out_dtype)
    return pl.pallas_call(
        _rmsnorm_kernel,
        grid=(m // rows,),
        in_specs=[pl.BlockSpec((rows, d), lambda i: (i, 0)),
                  pl.BlockSpec((1, d), lambda i: (0, 0))],
        out_specs=pl.BlockSpec((rows, d), lambda i: (i, 0)),
        out_shape=jax.ShapeDtypeStruct((m, d), out_dtype),
        compiler_params=_compiler_params(1, blocks, 2 * _nbytes((rows, d), F32)),
        name="rmsnorm",
    )(x, g.reshape(1, d).astype(F32))


def _matmul_kernel(*refs, nk, epilogue):
    if epilogue == "residual":
        x_ref, w_ref, r_ref, o_ref = refs
    else:
        x_ref, w_ref, o_ref = refs
        r_ref = None

    def product():
        return jnp.dot(x_ref[...], w_ref[...], preferred_element_type=F32)

    def first():
        acc = product()
        if epilogue == "relu2":
            z = jnp.maximum(acc, 0.0)
            acc = z * z
        elif epilogue == "residual":
            acc = r_ref[...] + acc
        o_ref[...] = acc.astype(o_ref.dtype)

    if nk == 1:
        first()
        return

    k = pl.program_id(2)
    pl.when(k == 0)(first)

    @pl.when(k > 0)
    def _():
        o_ref[...] += product()


def _matmul(x, w, layer, out_dtype, epilogue="none", res=None, bm=1024, bn=1024, bk=4096):
    m, kdim = x.shape
    n = w.shape[2]
    bm = _pick_tile(m, bm, SUBLANES)
    bn = _pick_tile(n, bn, MXU_WIDTH if n % MXU_WIDTH == 0 else LANES)
    bk = _pick_tile(kdim, bk, LANES)
    nk = kdim // bk
    assert nk == 1 or (out_dtype == F32 and epilogue != "relu2")
    in_specs = [pl.BlockSpec((bm, bk), lambda i, j, k: (i, k)),
                pl.BlockSpec((None, bk, bn), lambda i, j, k: (layer, k, j))]
    args = [x, w]
    blocks = (_nbytes((bm, bk), x.dtype) + _nbytes((bk, bn), w.dtype)
              + _nbytes((bm, bn), out_dtype))
    if epilogue == "residual":
        in_specs.append(pl.BlockSpec((bm, bn), lambda i, j, k: (i, j)))
        args.append(res)
        blocks += _nbytes((bm, bn), res.dtype)
    return pl.pallas_call(
        functools.partial(_matmul_kernel, nk=nk, epilogue=epilogue),
        grid=(m // bm, n // bn, nk),
        in_specs=in_specs,
        out_specs=pl.BlockSpec((bm, bn), lambda i, j, k: (i, j)),
        out_shape=jax.ShapeDtypeStruct((m, n), out_dtype),
        compiler_params=_compiler_params(3, blocks, 3 * _nbytes((bm, bn), F32)),
        name="matmul_" + epilogue,
    )(*args)


def _rope_table_kernel(pos_ref, freq_ref, cos_ref, sin_ref):
    ang = pos_ref[...].astype(F32) * freq_ref[...]
    cos_ref[...] = jnp.cos(ang)
    sin_ref[...] = jnp.sin(ang)


def _rope_tables(positions, half):
    m = positions.size
    rows = _pick_tile(m, 1024, SUBLANES)
    inv_freq = ROPE_THETA ** (-jnp.arange(half, dtype=F32) / half)
    out = jax.ShapeDtypeStruct((m, half), F32)
    return pl.pallas_call(
        _rope_table_kernel,
        grid=(m // rows,),
        in_specs=[pl.BlockSpec((rows, 1), lambda i: (i, 0)),
                  pl.BlockSpec((1, half), lambda i: (0, 0))],
        out_specs=[pl.BlockSpec((rows, half), lambda i: (i, 0))] * 2,
        out_shape=[out, out],
        compiler_params=_compiler_params(1, 3 * _nbytes((rows, LANES), F32),
                                         4 * _nbytes((rows, LANES), F32)),
        name="rope_tables",
    )(positions.reshape(m, 1), inv_freq.reshape(1, half))


def _retention_kernel(dec_ref, q_ref, k_ref, v_ref, g_ref, cos_ref, sin_ref, mask_ref,
                      qd_ref, kd_ref, o_ref, state_ref, *, n_sub, scale):
    head = pl.program_id(1)

    @pl.when(pl.program_id(2) == 0)
    def _():
        state_ref[...] = jnp.zeros_like(state_ref)

    dec = dec_ref[head]
    mask = mask_ref[0]
    qd = qd_ref[0]
    kd = kd_ref[0]
    sub = mask.shape[0]
    half = q_ref.shape[-1] // 2

    def rope(t, cos, sin):
        t1, t2 = t[:, :half], t[:, half:]
        return jnp.concatenate([t1 * cos - t2 * sin, t2 * cos + t1 * sin], axis=-1)

    for s in range(n_sub):
        rows = pl.ds(s * sub, sub)
        cos = cos_ref[rows, :]
        sin = sin_ref[rows, :]
        q = rope(q_ref[rows, :].astype(F32), cos, sin)
        k = rope(k_ref[rows, :].astype(F32), cos, sin) * scale
        v = v_ref[rows, :]
        qb = q.astype(MXU_DTYPE)
        kb = k.astype(MXU_DTYPE)
        scores = lax.dot_general(qb, kb, (((1,), (1,)), ((), ())),
                                 preferred_element_type=F32) * mask
        state = state_ref[...]
        o = (jnp.dot(scores.astype(MXU_DTYPE), v, preferred_element_type=F32)
             + jnp.dot(qb, state.astype(MXU_DTYPE), preferred_element_type=F32) * qd)
        kdb = (k * kd).astype(MXU_DTYPE)
        state_ref[...] = state * dec + lax.dot_general(
            kdb, v, (((0,), (0,)), ((), ())), preferred_element_type=F32)
        ms = jnp.mean(o * o, axis=-1, keepdims=True)
        of = o * lax.rsqrt(ms + EPS)
        g = g_ref[rows, :].astype(F32)
        o_ref[rows, :] = (of * (g * jax.nn.sigmoid(g))).astype(o_ref.dtype)


def _retention(proj, cos, sin, batch, seq, d_model):
    m = proj.shape[0]
    heads = RET_HEADS
    dk = d_model // heads
    dv = (proj.shape[1] - 2 * d_model) // (2 * heads)
    sub = _pick_tile(seq, RET_SUB, CHUNK)
    tt = _pick_tile(seq, 1024, sub)
    n_sub = tt // sub
    tpb = seq // tt
    v_blk0 = 2 * d_model // dv

    log_g = jnp.log1p(-jnp.exp2(-5.0 - jnp.arange(heads, dtype=F32)))
    idx = jnp.arange(sub, dtype=F32)
    dist = jnp.abs(idx[:, None] - idx[None, :])
    chunk_id = jnp.arange(sub) // CHUNK
    visible = chunk_id[None, :] <= chunk_id[:, None]
    mask = jnp.where(visible[None], jnp.exp(log_g[:, None, None] * dist[None]), 0.0)
    q_dec = jnp.exp(log_g[:, None] * (idx + 1.0))[:, :, None]
    k_dec = jnp.exp(log_g[:, None] * (sub - 1.0 - idx))[:, :, None]
    blk_dec = jnp.exp(log_g * sub)

    row = lambda b, h, t: b * tpb + t
    blocks = (2 * _nbytes((tt, dk), proj.dtype) + 2 * _nbytes((tt, dv), proj.dtype)
              + 2 * _nbytes((tt, dk // 2), F32)
              + _nbytes((sub, sub), F32) + 2 * _nbytes((sub, LANES), F32)
              + _nbytes((tt, dv), MXU_DTYPE))
    temps = _nbytes((dk, dv), F32) + 8 * _nbytes((sub, dv), F32)
    return pl.pallas_call(
        functools.partial(_retention_kernel, n_sub=n_sub, scale=dk ** -0.5),
        grid=(batch, heads, tpb),
        in_specs=[
            pl.BlockSpec(memory_space=pltpu.SMEM),
            pl.BlockSpec((tt, dk), lambda b, h, t: (row(b, h, t), h)),
            pl.BlockSpec((tt, dk), lambda b, h, t: (row(b, h, t), heads + h)),
            pl.BlockSpec((tt, dv), lambda b, h, t: (row(b, h, t), v_blk0 + h)),
            pl.BlockSpec((tt, dv), lambda b, h, t: (row(b, h, t), v_blk0 + heads + h)),
            pl.BlockSpec((tt, dk // 2), lambda b, h, t: (row(b, h, t), 0)),
            pl.BlockSpec((tt, dk // 2), lambda b, h, t: (row(b, h, t), 0)),
            pl.BlockSpec((1, sub, sub), lambda b, h, t: (h, 0, 0)),
            pl.BlockSpec((1, sub, 1), lambda b, h, t: (h, 0, 0)),
            pl.BlockSpec((1, sub, 1), lambda b, h, t: (h, 0, 0)),
        ],
        out_specs=pl.BlockSpec((tt, dv), lambda b, h, t: (row(b, h, t), h)),
        out_shape=jax.ShapeDtypeStruct((m, heads * dv), MXU_DTYPE),
        scratch_shapes=[pltpu.VMEM((dk, dv), F32)],
        compiler_params=_compiler_params(3, blocks, temps),
        name="retention",
    )(blk_dec, proj, proj, proj, proj, cos, sin, mask, q_dec, k_dec)


_P_CONV_B, _P_BIAS_R, _P_BIAS_I, _P_LAMBDA = CONV_WIDTH, CONV_WIDTH + 1, CONV_WIDTH + 2, CONV_WIDTH + 3


def _gate_windows(blk, group):
    width = min(group, _round_up(blk, LANES) + LANES)
    starts = tuple(min(n * blk // LANES * LANES, group - width) for n in range(group // blk))
    return width, starts


def _gelu_tanh(x):
    c = 0.7978845608028654
    return x * (0.5 * (1.0 + jnp.tanh(c * (x + 0.044715 * (x * x * x)))))


def _lru_kernel(x_ref, gb_ref, w_ref, p_ref, o_ref, xe_ref, xc_ref, xb_ref, gates_ref, h_ref,
                *, starts, width, chunk):
    tt, gw = x_ref.shape
    tail = SUBLANES
    n_tiles = tt // SUBLANES

    @pl.when(pl.program_id(2) == 0)
    def _():
        h_ref[...] = jnp.zeros_like(h_ref)
        xe_ref[0:tail, :] = jnp.zeros((tail, gw), F32)

    xe_ref[tail:tail + tt, :] = x_ref[...]
    p = p_ref[...]

    for c0 in range(0, gw, chunk):
        cols = slice(c0, c0 + chunk)
        xc = p[_P_CONV_B:_P_CONV_B + 1, cols]
        for k in range(CONV_WIDTH):
            xc = xc + p[k:k + 1, cols] * xe_ref[pl.ds(tail - (CONV_WIDTH - 1) + k, tt), cols]
        xc_ref[:, cols] = xc
        xb_ref[:, cols] = xc.astype(MXU_DTYPE)

    gates_ref[...] = jnp.zeros_like(gates_ref)
    for n, ws in enumerate(starts):
        win = slice(ws, ws + width)
        g = jnp.dot(xb_ref[:, win], w_ref[n], preferred_element_type=F32)
        gates_ref[:, win] += g[:, :width]
        gates_ref[:, gw + ws:gw + ws + width] += g[:, width:]

    for c0 in range(0, gw, chunk):
        cols = slice(c0, c0 + chunk)
        xc = xc_ref[:, cols]
        r = jax.nn.sigmoid(gates_ref[:, cols] + p[_P_BIAS_R:_P_BIAS_R + 1, cols])
        i = jax.nn.sigmoid(gates_ref[:, gw + c0:gw + c0 + chunk] + p[_P_BIAS_I:_P_BIAS_I + 1, cols])
        neg_lam = -p[_P_LAMBDA:_P_LAMBDA + 1, cols]
        softplus = jnp.maximum(neg_lam, 0.0) + jnp.log1p(jnp.exp(-jnp.abs(neg_lam)))
        log_a = (-LRU_C * r) * softplus
        a = jnp.exp(log_a)
        y = 1.0 - a * a
        u = (y * lax.rsqrt(jnp.maximum(y, SQRT_FLOOR))) * (i * xc)

        a3 = a.reshape(n_tiles, SUBLANES, chunk)
        u3 = u.reshape(n_tiles, SUBLANES, chunk)
        row = lax.broadcasted_iota(jnp.int32, a3.shape, 1)
        d = 1
        while d < SUBLANES:
            keep = row >= d
            a_prev = jnp.where(keep, pltpu.roll(a3, d, 1), 1.0)
            u_prev = jnp.where(keep, pltpu.roll(u3, d, 1), 0.0)
            u3 = u3 + a3 * u_prev
            a3 = a3 * a_prev
            d *= 2

        a_tot = jnp.broadcast_to(a3[:, SUBLANES - 1:SUBLANES, :], a3.shape)
        u_tot = jnp.broadcast_to(u3[:, SUBLANES - 1:SUBLANES, :], u3.shape)
        carry = h_ref[:, cols]
        h_in = []
        for t in range(n_tiles):
            h_in.append(carry)
            carry = u_tot[t] + a_tot[t] * carry
        h_ref[:, cols] = carry
        hs = (u3 + a3 * jnp.stack(h_in)).reshape(tt, chunk)
        o_ref[:, cols] = (hs * _gelu_tanh(gb_ref[:, cols])).astype(o_ref.dtype)

    xe_ref[0:tail, :] = xe_ref[tt:tt + tail, :]


def _lru(proj, w_win, params, batch, seq, group, width, starts):
    m = proj.shape[0]
    lru_width = proj.shape[1] // 2
    n_groups = lru_width // group
    bpg = len(starts)
    tt = _pick_tile(seq, LRU_ROWS, SUBLANES)
    tpb = seq // tt
    chunk = _pick_tile(group, 4 * LANES, LANES)
    row = lambda g, b, t: b * tpb + t
    blocks = (2 * _nbytes((tt, group), proj.dtype) + _nbytes((SUBLANES, group), F32)
              + _nbytes((tt, group), MXU_DTYPE))
    scratch = [pltpu.VMEM((tt + SUBLANES, group), F32),
               pltpu.VMEM((tt, group), F32),
               pltpu.VMEM((tt, group), MXU_DTYPE),
               pltpu.VMEM((tt, 2 * group), F32),
               pltpu.VMEM((SUBLANES, group), F32)]
    resident = (_nbytes((bpg, width, 2 * width), MXU_DTYPE) + _nbytes((tt + SUBLANES, group), F32)
                + _nbytes((tt, group), F32) + _nbytes((tt, group), MXU_DTYPE)
                + _nbytes((tt, 2 * group), F32) + 16 * _nbytes((tt, chunk), F32))
    return pl.pallas_call(
        functools.partial(_lru_kernel, starts=starts, width=width, chunk=chunk),
        grid=(n_groups, batch, tpb),
        in_specs=[
            pl.BlockSpec((tt, group), lambda g, b, t: (row(g, b, t), g)),
            pl.BlockSpec((tt, group), lambda g, b, t: (row(g, b, t), n_groups + g)),
            pl.BlockSpec((bpg, width, 2 * width), lambda g, b, t: (g, 0, 0),
                         pipeline_mode=pl.Buffered(1)),
            pl.BlockSpec((SUBLANES, group), lambda g, b, t: (0, g)),
        ],
        out_specs=pl.BlockSpec((tt, group), lambda g, b, t: (row(g, b, t), g)),
        out_shape=jax.ShapeDtypeStruct((m, lru_width), MXU_DTYPE),
        scratch_shapes=scratch,
        compiler_params=_compiler_params(3, blocks, resident),
        name="rglru",
    )(proj, proj, w_win, params)


def _embed_gate_weights(w_r, w_i, group, width, starts):
    blk = w_r.shape[1]
    bpg = len(starts)
    out = []
    for n in range(w_r.shape[0]):
        off = (n % bpg) * blk - starts[n % bpg]
        pad = ((off, width - blk - off), (off, width - blk - off))
        out.append(jnp.concatenate([jnp.pad(w_r[n], pad), jnp.pad(w_i[n], pad)], axis=1))
    return jnp.stack(out).astype(MXU_DTYPE)


def _mlp(h, g, w_up, w_down, layer):
    y = _rmsnorm(h, g, MXU_DTYPE)
    z = _matmul(y, w_up, layer, MXU_DTYPE, epilogue="relu2")
    return _matmul(z, w_down, layer, F32, epilogue="residual", res=h)


def _retention_layer(h, g, positions, w_in, w_out, layer, batch, seq):
    d = h.shape[1]
    y = _rmsnorm(h, g, MXU_DTYPE)
    proj = _matmul(y, w_in, layer, MXU_DTYPE)
    cos, sin = _rope_tables(positions, d // RET_HEADS // 2)
    o = _retention(proj, cos, sin, batch, seq, d)
    return _matmul(o, w_out, layer, F32, epilogue="residual", res=h)


def _lru_layer(h, g, w_in, conv_w, conv_b, w_rgate, b_rgate, w_igate, b_igate, lam, w_out,
               layer, batch, seq):
    lru_width = w_out.shape[1]
    blk = lru_width // LRU_BLOCKS
    group = blk * LANES // math.gcd(blk, LANES)
    width, starts = _gate_windows(blk, group)
    y = _rmsnorm(h, g, MXU_DTYPE)
    w_win = _embed_gate_weights(w_rgate, w_igate, group, width, starts)
    params = jnp.concatenate(
        [conv_w] + [v.reshape(1, lru_width) for v in (conv_b, b_rgate, b_igate, lam)],
        axis=0).astype(F32)
    proj = _matmul(y, w_in, layer, F32)
    mixed = _lru(proj, w_win, params, batch, seq, group, width, starts)
    return _matmul(mixed, w_out, layer, F32, epilogue="residual", res=h)


def kernel(x, positions, norm_mix_g, norm_mlp_g, final_norm_g, ret_w_in, ret_w_out,
           lru_w_in, lru_conv_w, lru_conv_b, lru_w_rgate, lru_b_rgate, lru_w_igate,
           lru_b_igate, lru_lambda, lru_w_out, mlp_w_up, mlp_w_down):
    batch, seq, d = x.shape
    h = x.reshape(batch * seq, d)
    depth = norm_mix_g.shape[0]
    n_mixers = 2
    cast = lambda w: w.astype(MXU_DTYPE)
    ret_w_in, ret_w_out, lru_w_in, lru_w_out, mlp_w_up, mlp_w_down = map(
        cast, (ret_w_in, ret_w_out, lru_w_in, lru_w_out, mlp_w_up, mlp_w_down))
    for i in range(depth):
        j = i // n_mixers
        if i % n_mixers == 0:
            h = _retention_layer(h, norm_mix_g[i], positions, ret_w_in, ret_w_out, j, batch, seq)
        else:
            h = _lru_layer(h, norm_mix_g[i], lru_w_in, lru_conv_w[j], lru_conv_b[j],
                           lru_w_rgate[j], lru_b_rgate[j], lru_w_igate[j], lru_b_igate[j],
                           lru_lambda[j], lru_w_out, j, batch, seq)
        h = _mlp(h, norm_mlp_g[i], mlp_w_up, mlp_w_down, i)
    return _rmsnorm(h, final_norm_g, x.dtype).reshape(batch, seq, d)
```

```python
import functools
import math

import jax
import jax.numpy as jnp
from jax import lax
from jax.experimental import pallas as pl
from jax.experimental.pallas import tpu as pltpu

RET_HEADS = 16
CHUNK = 64
ROPE_THETA = 10000.0
LRU_BLOCKS = 16
CONV_WIDTH = 4
LRU_C = 8.0
EPS = 1e-6

LANES = 128
SUBLANES = 8
MXU_WIDTH = 256
VMEM_BYTES_V7X = 64 * 1024 * 1024

MXU_DTYPE = jnp.bfloat16
F32 = jnp.float32

RET_SUB = 256
LRU_ROWS = 256
SQRT_FLOOR = 1e-30


def _round_up(x, m):
    return (x + m - 1) // m * m


def _pick_tile(dim, target, align):
    if dim <= target:
        return dim
    t = target // align * align
    while t >= align:
        if dim % t == 0:
            return t
        t -= align
    return dim


def _nbytes(shape, dtype):
    n = 1
    for s in shape:
        n *= s
    return n * jnp.dtype(dtype).itemsize


def _compiler_params(n_axes, pipelined_bytes, resident_bytes):
    limit = 2 * pipelined_bytes + resident_bytes
    limit = min(max(limit, 16 * 1024 * 1024), VMEM_BYTES_V7X - 4 * 1024 * 1024)
    return pltpu.CompilerParams(
        dimension_semantics=("arbitrary",) * n_axes, vmem_limit_bytes=int(limit))


def _rmsnorm_kernel(x_ref, g_ref, o_ref):
    x = x_ref[...]
    ms = jnp.mean(x * x, axis=-1, keepdims=True)
    o_ref[...] = ((x * lax.rsqrt(ms + EPS)) * g_ref[...]).astype(o_ref.dtype)


def _rmsnorm(x, g, out_dtype):
    m, d = x.shape
    rows = _pick_tile(m, 256, SUBLANES)
    blocks = _nbytes((rows, d), x.dtype) + _nbytes((rows, d), out_dtype)
    return pl.pallas_call(
        _rmsnorm_kernel,
        grid=(m // rows,),
        in_specs=[pl.BlockSpec((rows, d), lambda i: (i, 0)),
                  pl.BlockSpec((1, d), lambda i: (0, 0))],
        out_specs=pl.BlockSpec((rows, d), lambda i: (i, 0)),
        out_shape=jax.ShapeDtypeStruct((m, d), out_dtype),
        compiler_params=_compiler_params(1, blocks, 2 * _nbytes((rows, d), F32)),
        name="rmsnorm",
    )(x, g.reshape(1, d).astype(F32))


def _row_sumsq(x):
    return jnp.broadcast_to(jnp.sum(x * x, axis=-1, keepdims=True), (x.shape[0], LANES))


def _norm_operand_kernel(x_ref, g_ref, xg_ref, ss_ref):
    x = x_ref[...]
    xg_ref[...] = (x * g_ref[...]).astype(xg_ref.dtype)
    ss_ref[...] = _row_sumsq(x)


def _norm_operand(x, g):
    m, d = x.shape
    rows = _pick_tile(m, 256, SUBLANES)
    blocks = (_nbytes((rows, d), x.dtype) + _nbytes((rows, d), MXU_DTYPE)
              + _nbytes((rows, LANES), F32))
    return pl.pallas_call(
        _norm_operand_kernel,
        grid=(m // rows,),
        in_specs=[pl.BlockSpec((rows, d), lambda i: (i, 0)),
                  pl.BlockSpec((1, d), lambda i: (0, 0))],
        out_specs=[pl.BlockSpec((rows, d), lambda i: (i, 0)),
                   pl.BlockSpec((rows, LANES), lambda i: (i, 0))],
        out_shape=[jax.ShapeDtypeStruct((m, d), MXU_DTYPE),
                   jax.ShapeDtypeStruct((m, LANES), F32)],
        compiler_params=_compiler_params(1, blocks, 2 * _nbytes((rows, d), F32)),
        name="norm_operand",
    )(x, g.reshape(1, d).astype(F32))


def _matmul_kernel(*refs, nk, epilogue, scaled, emit_norm, inv_dim):
    refs = iter(refs)
    x_ref, w_ref = next(refs), next(refs)
    ss_ref = next(refs) if scaled else None
    r_ref = next(refs) if epilogue == "residual" else None
    g_ref = next(refs) if emit_norm else None
    o_ref = next(refs)
    og_ref, oss_ref = (next(refs), next(refs)) if emit_norm else (None, None)

    def product():
        acc = jnp.dot(x_ref[...], w_ref[...], preferred_element_type=F32)
        if scaled:
            ss = ss_ref[...]
            total = ss[:, :LANES]
            for j in range(1, ss.shape[1] // LANES):
                total = total + ss[:, j * LANES:(j + 1) * LANES]
            scale = lax.rsqrt(total * inv_dim + EPS)
            acc = acc * jnp.tile(scale, (1, acc.shape[1] // LANES))
        return acc

    def store(out):
        o_ref[...] = out.astype(o_ref.dtype)
        if emit_norm:
            og_ref[...] = (out * g_ref[...]).astype(og_ref.dtype)
            oss_ref[...] = _row_sumsq(out)

    def first(last):
        acc = product()
        if epilogue == "relu2":
            z = jnp.maximum(acc, 0.0)
            acc = z * z
        elif epilogue == "residual":
            acc = r_ref[...] + acc
        if last:
            store(acc)
        else:
            o_ref[...] = acc

    if nk == 1:
        first(True)
        return

    k = pl.program_id(2)
    pl.when(k == 0)(functools.partial(first, False))

    if not emit_norm:
        @pl.when(k > 0)
        def _():
            o_ref[...] += product()
        return

    if nk > 2:
        @pl.when(jnp.logical_and(k > 0, k < nk - 1))
        def _():
            o_ref[...] += product()

    @pl.when(k == nk - 1)
    def _():
        store(o_ref[...] + product())


def _matmul(x, w, layer, out_dtype, epilogue="none", res=None, row_sumsq=None, next_gain=None,
            bm=1024, bn=1024, bk=4096):
    m, kdim = x.shape
    n = w.shape[2]
    bm = _pick_tile(m, bm, SUBLANES)
    bn = _pick_tile(n, bn, MXU_WIDTH if n % MXU_WIDTH == 0 else LANES)
    bk = _pick_tile(kdim, bk, LANES)
    nk = kdim // bk
    scaled = row_sumsq is not None
    emit_norm = next_gain is not None
    assert nk == 1 or (out_dtype == F32 and epilogue != "relu2" and not scaled)
    in_specs = [pl.BlockSpec((bm, bk), lambda i, j, k: (i, k)),
                pl.BlockSpec((None, bk, bn), lambda i, j, k: (layer, k, j))]
    args = [x, w]
    blocks = (_nbytes((bm, bk), x.dtype) + _nbytes((bk, bn), w.dtype)
              + _nbytes((bm, bn), out_dtype))
    if scaled:
        in_specs.append(pl.BlockSpec((bm, row_sumsq.shape[1]), lambda i, j, k: (i, 0)))
        args.append(row_sumsq)
        blocks += _nbytes((bm, row_sumsq.shape[1]), F32)
    if epilogue == "residual":
        in_specs.append(pl.BlockSpec((bm, bn), lambda i, j, k: (i, j)))
        args.append(res)
        blocks += _nbytes((bm, bn), res.dtype)
    out_specs = pl.BlockSpec((bm, bn), lambda i, j, k: (i, j))
    out_shape = jax.ShapeDtypeStruct((m, n), out_dtype)
    if emit_norm:
        in_specs.append(pl.BlockSpec((1, bn), lambda i, j, k: (0, j)))
        args.append(next_gain.reshape(1, n).astype(F32))
        out_specs = [out_specs, pl.BlockSpec((bm, bn), lambda i, j, k: (i, j)),
                     pl.BlockSpec((bm, LANES), lambda i, j, k: (i, j))]
        out_shape = [out_shape, jax.ShapeDtypeStruct((m, n), MXU_DTYPE),
                     jax.ShapeDtypeStruct((m, n // bn * LANES), F32)]
        blocks += _nbytes((bm, bn), MXU_DTYPE) + _nbytes((bm, LANES), F32)
    return pl.pallas_call(
        functools.partial(_matmul_kernel, nk=nk, epilogue=epilogue, scaled=scaled,
                          emit_norm=emit_norm, inv_dim=1.0 / kdim),
        grid=(m // bm, n // bn, nk),
        in_specs=in_specs,
        out_specs=out_specs,
        out_shape=out_shape,
        compiler_params=_compiler_params(3, blocks, 3 * _nbytes((bm, bn), F32)),
        name="matmul_" + epilogue,
    )(*args)


def _rope_table_kernel(pos_ref, freq_ref, cos_ref, sin_ref):
    ang = pos_ref[...].astype(F32) * freq_ref[...]
    cos_ref[...] = jnp.cos(ang)
    sin_ref[...] = jnp.sin(ang)


def _rope_tables(positions, half):
    m = positions.size
    rows = _pick_tile(m, 1024, SUBLANES)
    inv_freq = ROPE_THETA ** (-jnp.arange(half, dtype=F32) / half)
    out = jax.ShapeDtypeStruct((m, half), F32)
    return pl.pallas_call(
        _rope_table_kernel,
        grid=(m // rows,),
        in_specs=[pl.BlockSpec((rows, 1), lambda i: (i, 0)),
                  pl.BlockSpec((1, half), lambda i: (0, 0))],
        out_specs=[pl.BlockSpec((rows, half), lambda i: (i, 0))] * 2,
        out_shape=[out, out],
        compiler_params=_compiler_params(1, 3 * _nbytes((rows, LANES), F32),
                                         4 * _nbytes((rows, LANES), F32)),
        name="rope_tables",
    )(positions.reshape(m, 1), inv_freq.reshape(1, half))


def _retention_kernel(dec_ref, q_ref, k_ref, v_ref, g_ref, cos_ref, sin_ref, mask_ref,
                      qd_ref, kd_ref, o_ref, state_ref, *, n_sub, scale):
    head = pl.program_id(1)

    @pl.when(pl.program_id(2) == 0)
    def _():
        state_ref[...] = jnp.zeros_like(state_ref)

    dec = dec_ref[head]
    mask = mask_ref[0]
    qd = qd_ref[0]
    kd = kd_ref[0]
    sub = mask.shape[0]
    half = q_ref.shape[-1] // 2

    def rope(t, cos, sin):
        t1, t2 = t[:, :half], t[:, half:]
        return jnp.concatenate([t1 * cos - t2 * sin, t2 * cos + t1 * sin], axis=-1)

    for s in range(n_sub):
        rows = pl.ds(s * sub, sub)
        cos = cos_ref[rows, :]
        sin = sin_ref[rows, :]
        q = rope(q_ref[rows, :].astype(F32), cos, sin)
        k = rope(k_ref[rows, :].astype(F32), cos, sin) * scale
        v = v_ref[rows, :]
        qb = q.astype(MXU_DTYPE)
        kb = k.astype(MXU_DTYPE)
        scores = lax.dot_general(qb, kb, (((1,), (1,)), ((), ())),
                                 preferred_element_type=F32) * mask
        state = state_ref[...]
        o = (jnp.dot(scores.astype(MXU_DTYPE), v, preferred_element_type=F32)
             + jnp.dot(qb, state.astype(MXU_DTYPE), preferred_element_type=F32) * qd)
        kdb = (k * kd).astype(MXU_DTYPE)
        state_ref[...] = state * dec + lax.dot_general(
            kdb, v, (((0,), (0,)), ((), ())), preferred_element_type=F32)
        ms = jnp.mean(o * o, axis=-1, keepdims=True)
        of = o * lax.rsqrt(ms + EPS)
        g = g_ref[rows, :].astype(F32)
        o_ref[rows, :] = (of * (g * jax.nn.sigmoid(g))).astype(o_ref.dtype)


def _retention(proj, cos, sin, batch, seq, d_model):
    m = proj.shape[0]
    heads = RET_HEADS
    dk = d_model // heads
    dv = (proj.shape[1] - 2 * d_model) // (2 * heads)
    sub = _pick_tile(seq, RET_SUB, CHUNK)
    tt = _pick_tile(seq, 1024, sub)
    n_sub = tt // sub
    tpb = seq // tt
    v_blk0 = 2 * d_model // dv

    log_g = jnp.log1p(-jnp.exp2(-5.0 - jnp.arange(heads, dtype=F32)))
    idx = jnp.arange(sub, dtype=F32)
    dist = jnp.abs(idx[:, None] - idx[None, :])
    chunk_id = jnp.arange(sub) // CHUNK
    visible = chunk_id[None, :] <= chunk_id[:, None]
    mask = jnp.where(visible[None], jnp.exp(log_g[:, None, None] * dist[None]), 0.0)
    q_dec = jnp.exp(log_g[:, None] * (idx + 1.0))[:, :, None]
    k_dec = jnp.exp(log_g[:, None] * (sub - 1.0 - idx))[:, :, None]
    blk_dec = jnp.exp(log_g * sub)

    row = lambda b, h, t: b * tpb + t
    blocks = (2 * _nbytes((tt, dk), proj.dtype) + 2 * _nbytes((tt, dv), proj.dtype)
              + 2 * _nbytes((tt, dk // 2), F32)
              + _nbytes((sub, sub), F32) + 2 * _nbytes((sub, LANES), F32)
              + _nbytes((tt, dv), MXU_DTYPE))
    temps = _nbytes((dk, dv), F32) + 8 * _nbytes((sub, dv), F32)
    return pl.pallas_call(
        functools.partial(_retention_kernel, n_sub=n_sub, scale=dk ** -0.5),
        grid=(batch, heads, tpb),
        in_specs=[
            pl.BlockSpec(memory_space=pltpu.SMEM),
            pl.BlockSpec((tt, dk), lambda b, h, t: (row(b, h, t), h)),
            pl.BlockSpec((tt, dk), lambda b, h, t: (row(b, h, t), heads + h)),
            pl.BlockSpec((tt, dv), lambda b, h, t: (row(b, h, t), v_blk0 + h)),
            pl.BlockSpec((tt, dv), lambda b, h, t: (row(b, h, t), v_blk0 + heads + h)),
            pl.BlockSpec((tt, dk // 2), lambda b, h, t: (row(b, h, t), 0)),
            pl.BlockSpec((tt, dk // 2), lambda b, h, t: (row(b, h, t), 0)),
            pl.BlockSpec((1, sub, sub), lambda b, h, t: (h, 0, 0)),
            pl.BlockSpec((1, sub, 1), lambda b, h, t: (h, 0, 0)),
            pl.BlockSpec((1, sub, 1), lambda b, h, t: (h, 0, 0)),
        ],
        out_specs=pl.BlockSpec((tt, dv), lambda b, h, t: (row(b, h, t), h)),
        out_shape=jax.ShapeDtypeStruct((m, heads * dv), MXU_DTYPE),
        scratch_shapes=[pltpu.VMEM((dk, dv), F32)],
        compiler_params=_compiler_params(3, blocks, temps),
        name="retention",
    )(blk_dec, proj, proj, proj, proj, cos, sin, mask, q_dec, k_dec)


_P_CONV_B, _P_BIAS_R, _P_BIAS_I, _P_LAMBDA = CONV_WIDTH, CONV_WIDTH + 1, CONV_WIDTH + 2, CONV_WIDTH + 3


def _gate_windows(blk, group):
    width = min(group, _round_up(blk, LANES) + LANES)
    starts = tuple(min(n * blk // LANES * LANES, group - width) for n in range(group // blk))
    return width, starts


def _gelu_tanh(x):
    c = 0.7978845608028654
    return x * (0.5 * (1.0 + jnp.tanh(c * (x + 0.044715 * (x * x * x)))))


def _lru_kernel(x_ref, gb_ref, w_ref, p_ref, o_ref, xe_ref, xc_ref, xb_ref, gates_ref, h_ref,
                *, starts, width, chunk):
    tt, gw = x_ref.shape
    tail = SUBLANES
    n_tiles = tt // SUBLANES

    @pl.when(pl.program_id(2) == 0)
    def _():
        h_ref[...] = jnp.zeros_like(h_ref)
        xe_ref[0:tail, :] = jnp.zeros((tail, gw), F32)

    xe_ref[tail:tail + tt, :] = x_ref[...]
    p = p_ref[...]

    for c0 in range(0, gw, chunk):
        cols = slice(c0, c0 + chunk)
        xh = xe_ref[:, cols]
        xc = p[_P_CONV_B:_P_CONV_B + 1, cols]
        for k in range(CONV_WIDTH):
            lag = CONV_WIDTH - 1 - k
            xk = pltpu.roll(xh, lag, 0) if lag else xh
            xc = xc + p[k:k + 1, cols] * xk[tail:, :]
        xc_ref[:, cols] = xc
        xb_ref[:, cols] = xc.astype(MXU_DTYPE)

    gates_ref[...] = jnp.zeros_like(gates_ref)
    for n, ws in enumerate(starts):
        win = slice(ws, ws + width)
        g = jnp.dot(xb_ref[:, win], w_ref[n], preferred_element_type=F32)
        gates_ref[:, win] += g[:, :width]
        gates_ref[:, gw + ws:gw + ws + width] += g[:, width:]

    for c0 in range(0, gw, chunk):
        cols = slice(c0, c0 + chunk)
        xc = xc_ref[:, cols]
        r = jax.nn.sigmoid(gates_ref[:, cols] + p[_P_BIAS_R:_P_BIAS_R + 1, cols])
        i = jax.nn.sigmoid(gates_ref[:, gw + c0:gw + c0 + chunk] + p[_P_BIAS_I:_P_BIAS_I + 1, cols])
        neg_lam = -p[_P_LAMBDA:_P_LAMBDA + 1, cols]
        softplus = jnp.maximum(neg_lam, 0.0) + jnp.log1p(jnp.exp(-jnp.abs(neg_lam)))
        log_a = (-LRU_C * r) * softplus
        a = jnp.exp(log_a)
        y = 1.0 - a * a
        u = (y * lax.rsqrt(jnp.maximum(y, SQRT_FLOOR))) * (i * xc)

        a3 = a.reshape(n_tiles, SUBLANES, chunk)
        u3 = u.reshape(n_tiles, SUBLANES, chunk)
        row = lax.broadcasted_iota(jnp.int32, a3.shape, 1)
        d = 1
        while d < SUBLANES:
            keep = row >= d
            a_prev = jnp.where(keep, pltpu.roll(a3, d, 1), 1.0)
            u_prev = jnp.where(keep, pltpu.roll(u3, d, 1), 0.0)
            u3 = u3 + a3 * u_prev
            a3 = a3 * a_prev
            d *= 2

        a_tot = jnp.broadcast_to(a3[:, SUBLANES - 1:SUBLANES, :], a3.shape)
        u_tot = jnp.broadcast_to(u3[:, SUBLANES - 1:SUBLANES, :], u3.shape)
        carry = h_ref[:, cols]
        h_in = []
        for t in range(n_tiles):
            h_in.append(carry)
            carry = u_tot[t] + a_tot[t] * carry
        h_ref[:, cols] = carry
        hs = (u3 + a3 * jnp.stack(h_in)).reshape(tt, chunk)
        o_ref[:, cols] = (hs * _gelu_tanh(gb_ref[:, cols])).astype(o_ref.dtype)

    xe_ref[0:tail, :] = xe_ref[tt:tt + tail, :]


def _lru(proj, w_win, params, batch, seq, group, width, starts):
    m = proj.shape[0]
    lru_width = proj.shape[1] // 2
    n_groups = lru_width // group
    bpg = len(starts)
    tt = _pick_tile(seq, LRU_ROWS, SUBLANES)
    tpb = seq // tt
    chunk = _pick_tile(group, 4 * LANES, LANES)
    row = lambda g, b, t: b * tpb + t
    blocks = (2 * _nbytes((tt, group), proj.dtype) + _nbytes((SUBLANES, group), F32)
              + _nbytes((tt, group), MXU_DTYPE))
    scratch = [pltpu.VMEM((tt + SUBLANES, group), F32),
               pltpu.VMEM((tt, group), F32),
               pltpu.VMEM((tt, group), MXU_DTYPE),
               pltpu.VMEM((tt, 2 * group), F32),
               pltpu.VMEM((SUBLANES, group), F32)]
    resident = (_nbytes((bpg, width, 2 * width), MXU_DTYPE) + _nbytes((tt + SUBLANES, group), F32)
                + _nbytes((tt, group), F32) + _nbytes((tt, group), MXU_DTYPE)
                + _nbytes((tt, 2 * group), F32) + 16 * _nbytes((tt, chunk), F32))
    return pl.pallas_call(
        functools.partial(_lru_kernel, starts=starts, width=width, chunk=chunk),
        grid=(n_groups, batch, tpb),
        in_specs=[
            pl.BlockSpec((tt, group), lambda g, b, t: (row(g, b, t), g)),
            pl.BlockSpec((tt, group), lambda g, b, t: (row(g, b, t), n_groups + g)),
            pl.BlockSpec((bpg, width, 2 * width), lambda g, b, t: (g, 0, 0),
                         pipeline_mode=pl.Buffered(1)),
            pl.BlockSpec((SUBLANES, group), lambda g, b, t: (0, g)),
        ],
        out_specs=pl.BlockSpec((tt, group), lambda g, b, t: (row(g, b, t), g)),
        out_shape=jax.ShapeDtypeStruct((m, lru_width), MXU_DTYPE),
        scratch_shapes=scratch,
        compiler_params=_compiler_params(3, blocks, resident),
        name="rglru",
    )(proj, proj, w_win, params)


def _embed_gate_weights(w_r, w_i, group, width, starts):
    blk = w_r.shape[1]
    bpg = len(starts)
    out = []
    for n in range(w_r.shape[0]):
        off = (n % bpg) * blk - starts[n % bpg]
        pad = ((off, width - blk - off), (off, width - blk - off))
        out.append(jnp.concatenate([jnp.pad(w_r[n], pad), jnp.pad(w_i[n], pad)], axis=1))
    return jnp.stack(out).astype(MXU_DTYPE)


_NORM_BK = 2048


def _mlp(h, hg, ss, w_up, w_down, layer, next_gain):
    z = _matmul(hg, w_up, layer, MXU_DTYPE, epilogue="relu2", row_sumsq=ss)
    if next_gain is None:
        return _matmul(z, w_down, layer, F32, epilogue="residual", res=h), None, None
    return _matmul(z, w_down, layer, F32, epilogue="residual", res=h, next_gain=next_gain,
                   bk=_NORM_BK)


def _retention_layer(h, hg, ss, positions, w_in, w_out, layer, next_gain, batch, seq):
    d = h.shape[1]
    proj = _matmul(hg, w_in, layer, MXU_DTYPE, row_sumsq=ss)
    cos, sin = _rope_tables(positions, d // RET_HEADS // 2)
    o = _retention(proj, cos, sin, batch, seq, d)
    return _matmul(o, w_out, layer, F32, epilogue="residual", res=h, next_gain=next_gain,
                   bk=_NORM_BK)


def _lru_layer(h, hg, ss, w_in, conv_w, conv_b, w_rgate, b_rgate, w_igate, b_igate, lam, w_out,
               layer, next_gain, batch, seq):
    lru_width = w_out.shape[1]
    blk = lru_width // LRU_BLOCKS
    group = blk * LANES // math.gcd(blk, LANES)
    width, starts = _gate_windows(blk, group)
    w_win = _embed_gate_weights(w_rgate, w_igate, group, width, starts)
    params = jnp.concatenate(
        [conv_w] + [v.reshape(1, lru_width) for v in (conv_b, b_rgate, b_igate, lam)],
        axis=0).astype(F32)
    proj = _matmul(hg, w_in, layer, F32, row_sumsq=ss)
    mixed = _lru(proj, w_win, params, batch, seq, group, width, starts)
    return _matmul(mixed, w_out, layer, F32, epilogue="residual", res=h, next_gain=next_gain,
                   bk=_NORM_BK)


def kernel(x, positions, norm_mix_g, norm_mlp_g, final_norm_g, ret_w_in, ret_w_out,
           lru_w_in, lru_conv_w, lru_conv_b, lru_w_rgate, lru_b_rgate, lru_w_igate,
           lru_b_igate, lru_lambda, lru_w_out, mlp_w_up, mlp_w_down):
    batch, seq, d = x.shape
    h = x.reshape(batch * seq, d)
    depth = norm_mix_g.shape[0]
    n_mixers = 2
    cast = lambda w: w.astype(MXU_DTYPE)
    ret_w_in, ret_w_out, lru_w_in, lru_w_out, mlp_w_up, mlp_w_down = map(
        cast, (ret_w_in, ret_w_out, lru_w_in, lru_w_out, mlp_w_up, mlp_w_down))
    hg, ss = _norm_operand(h, norm_mix_g[0])
    for i in range(depth):
        j = i // n_mixers
        if i % n_mixers == 0:
            h, hg, ss = _retention_layer(h, hg, ss, positions, ret_w_in, ret_w_out, j,
                                         norm_mlp_g[i], batch, seq)
        else:
            h, hg, ss = _lru_layer(h, hg, ss, lru_w_in, lru_conv_w[j], lru_conv_b[j],
                                   lru_w_rgate[j], lru_b_rgate[j], lru_w_igate[j],
                                   lru_b_igate[j], lru_lambda[j], lru_w_out, j,
                                   norm_mlp_g[i], batch, seq)
        next_gain = norm_mix_g[i + 1] if i + 1 < depth else None
        h, hg, ss = _mlp(h, hg, ss, mlp_w_up, mlp_w_down, i, next_gain)
    return _rmsnorm(h, final_norm_g, x.dtype).reshape(batch, seq, d)
```

```python
import functools
import math

import jax
import jax.numpy as jnp
from jax import lax
from jax.experimental import pallas as pl
from jax.experimental.pallas import tpu as pltpu

RET_HEADS = 16
CHUNK = 64
ROPE_THETA = 10000.0
LRU_BLOCKS = 16
CONV_WIDTH = 4
LRU_C = 8.0
EPS = 1e-6

LANES = 128
SUBLANES = 8
MXU_WIDTH = 256
VMEM_BYTES_V7X = 64 * 1024 * 1024
VMEM_COMPILER_RESERVE = 2 * 1024 * 1024
VMEM_LIMIT_FLOOR = 16 * 1024 * 1024

MXU_DTYPE = jnp.bfloat16
F32 = jnp.float32

RET_SUB = 256
LRU_ROWS = 256
SQRT_FLOOR = 1e-30


def _round_up(x, m):
    return (x + m - 1) // m * m


def _pick_tile(dim, target, align):
    if dim <= target:
        return dim
    t = target // align * align
    while t >= align:
        if dim % t == 0:
            return t
        t -= align
    return dim


def _nbytes(shape, dtype):
    n = 1
    for s in shape:
        n *= s
    return n * jnp.dtype(dtype).itemsize


def _compiler_params(n_axes, pipelined_bytes, resident_bytes):
    limit = 2 * pipelined_bytes + resident_bytes
    limit = min(max(limit, VMEM_LIMIT_FLOOR), VMEM_BYTES_V7X - VMEM_COMPILER_RESERVE)
    return pltpu.CompilerParams(
        dimension_semantics=("arbitrary",) * n_axes, vmem_limit_bytes=int(limit))


def _rmsnorm_kernel(x_ref, g_ref, o_ref):
    x = x_ref[...]
    ms = jnp.mean(x * x, axis=-1, keepdims=True)
    o_ref[...] = ((x * lax.rsqrt(ms + EPS)) * g_ref[...]).astype(o_ref.dtype)


def _rmsnorm(x, g, out_dtype):
    m, d = x.shape
    rows = _pick_tile(m, 256, SUBLANES)
    blocks = _nbytes((rows, d), x.dtype) + _nbytes((rows, d), out_dtype)
    return pl.pallas_call(
        _rmsnorm_kernel,
        grid=(m // rows,),
        in_specs=[pl.BlockSpec((rows, d), lambda i: (i, 0)),
                  pl.BlockSpec((1, d), lambda i: (0, 0))],
        out_specs=pl.BlockSpec((rows, d), lambda i: (i, 0)),
        out_shape=jax.ShapeDtypeStruct((m, d), out_dtype),
        compiler_params=_compiler_params(1, blocks, 2 * _nbytes((rows, d), F32)),
        name="rmsnorm",
    )(x, g.reshape(1, d).astype(F32))


def _row_sumsq(x):
    return jnp.broadcast_to(jnp.sum(x * x, axis=-1, keepdims=True), (x.shape[0], LANES))


def _norm_operand_kernel(x_ref, g_ref, xg_ref, ss_ref):
    x = x_ref[...]
    xg_ref[...] = (x * g_ref[...]).astype(xg_ref.dtype)
    ss_ref[...] = _row_sumsq(x)


def _norm_operand(x, g):
    m, d = x.shape
    rows = _pick_tile(m, 256, SUBLANES)
    blocks = (_nbytes((rows, d), x.dtype) + _nbytes((rows, d), MXU_DTYPE)
              + _nbytes((rows, LANES), F32))
    return pl.pallas_call(
        _norm_operand_kernel,
        grid=(m // rows,),
        in_specs=[pl.BlockSpec((rows, d), lambda i: (i, 0)),
                  pl.BlockSpec((1, d), lambda i: (0, 0))],
        out_specs=[pl.BlockSpec((rows, d), lambda i: (i, 0)),
                   pl.BlockSpec((rows, LANES), lambda i: (i, 0))],
        out_shape=[jax.ShapeDtypeStruct((m, d), MXU_DTYPE),
                   jax.ShapeDtypeStruct((m, LANES), F32)],
        compiler_params=_compiler_params(1, blocks, 2 * _nbytes((rows, d), F32)),
        name="norm_operand",
    )(x, g.reshape(1, d).astype(F32))


def _matmul_kernel(*refs, nk, epilogue, scaled, emit_norm, inv_dim):
    refs = iter(refs)
    x_ref, w_ref = next(refs), next(refs)
    ss_ref = next(refs) if scaled else None
    r_ref = next(refs) if epilogue == "residual" else None
    g_ref = next(refs) if emit_norm else None
    o_ref = next(refs)
    og_ref, oss_ref = (next(refs), next(refs)) if emit_norm else (None, None)

    def product():
        acc = jnp.dot(x_ref[...], w_ref[...], preferred_element_type=F32)
        if scaled:
            ss = ss_ref[...]
            total = ss[:, :LANES]
            for j in range(1, ss.shape[1] // LANES):
                total = total + ss[:, j * LANES:(j + 1) * LANES]
            scale = lax.rsqrt(total * inv_dim + EPS)
            acc = acc * jnp.tile(scale, (1, acc.shape[1] // LANES))
        return acc

    def store(out):
        o_ref[...] = out.astype(o_ref.dtype)
        if emit_norm:
            og_ref[...] = (out * g_ref[...]).astype(og_ref.dtype)
            oss_ref[...] = _row_sumsq(out)

    def first(last):
        acc = product()
        if epilogue == "relu2":
            z = jnp.maximum(acc, 0.0)
            acc = z * z
        elif epilogue == "residual":
            acc = r_ref[...] + acc
        if last:
            store(acc)
        else:
            o_ref[...] = acc

    if nk == 1:
        first(True)
        return

    k = pl.program_id(2)
    pl.when(k == 0)(functools.partial(first, False))

    if not emit_norm:
        @pl.when(k > 0)
        def _():
            o_ref[...] += product()
        return

    if nk > 2:
        @pl.when(jnp.logical_and(k > 0, k < nk - 1))
        def _():
            o_ref[...] += product()

    @pl.when(k == nk - 1)
    def _():
        store(o_ref[...] + product())


def _matmul(x, w, layer, out_dtype, epilogue="none", res=None, row_sumsq=None, next_gain=None,
            bm=1024, bn=1024, bk=4096):
    m, kdim = x.shape
    n = w.shape[2]
    bm = _pick_tile(m, bm, SUBLANES)
    bn = _pick_tile(n, bn, MXU_WIDTH if n % MXU_WIDTH == 0 else LANES)
    bk = _pick_tile(kdim, bk, LANES)
    nk = kdim // bk
    scaled = row_sumsq is not None
    emit_norm = next_gain is not None
    assert nk == 1 or (out_dtype == F32 and epilogue != "relu2" and not scaled)
    in_specs = [pl.BlockSpec((bm, bk), lambda i, j, k: (i, k)),
                pl.BlockSpec((None, bk, bn), lambda i, j, k: (layer, k, j))]
    args = [x, w]
    blocks = (_nbytes((bm, bk), x.dtype) + _nbytes((bk, bn), w.dtype)
              + _nbytes((bm, bn), out_dtype))
    if scaled:
        in_specs.append(pl.BlockSpec((bm, row_sumsq.shape[1]), lambda i, j, k: (i, 0)))
        args.append(row_sumsq)
        blocks += _nbytes((bm, row_sumsq.shape[1]), F32)
    if epilogue == "residual":
        in_specs.append(pl.BlockSpec((bm, bn), lambda i, j, k: (i, j)))
        args.append(res)
        blocks += _nbytes((bm, bn), res.dtype)
    out_specs = pl.BlockSpec((bm, bn), lambda i, j, k: (i, j))
    out_shape = jax.ShapeDtypeStruct((m, n), out_dtype)
    if emit_norm:
        in_specs.append(pl.BlockSpec((1, bn), lambda i, j, k: (0, j)))
        args.append(next_gain.reshape(1, n).astype(F32))
        out_specs = [out_specs, pl.BlockSpec((bm, bn), lambda i, j, k: (i, j)),
                     pl.BlockSpec((bm, LANES), lambda i, j, k: (i, j))]
        out_shape = [out_shape, jax.ShapeDtypeStruct((m, n), MXU_DTYPE),
                     jax.ShapeDtypeStruct((m, n // bn * LANES), F32)]
        blocks += _nbytes((bm, bn), MXU_DTYPE) + _nbytes((bm, LANES), F32)
    return pl.pallas_call(
        functools.partial(_matmul_kernel, nk=nk, epilogue=epilogue, scaled=scaled,
                          emit_norm=emit_norm, inv_dim=1.0 / kdim),
        grid=(m // bm, n // bn, nk),
        in_specs=in_specs,
        out_specs=out_specs,
        out_shape=out_shape,
        compiler_params=_compiler_params(3, blocks, 3 * _nbytes((bm, bn), F32)),
        name="matmul_" + epilogue,
    )(*args)


def _rope_table_kernel(pos_ref, freq_ref, cos_ref, sin_ref):
    ang = pos_ref[...].astype(F32) * freq_ref[...]
    cos_ref[...] = jnp.cos(ang)
    sin_ref[...] = jnp.sin(ang)


def _rope_tables(positions, half):
    m = positions.size
    rows = _pick_tile(m, 1024, SUBLANES)
    inv_freq = ROPE_THETA ** (-jnp.arange(half, dtype=F32) / half)
    out = jax.ShapeDtypeStruct((m, half), F32)
    return pl.pallas_call(
        _rope_table_kernel,
        grid=(m // rows,),
        in_specs=[pl.BlockSpec((rows, 1), lambda i: (i, 0)),
                  pl.BlockSpec((1, half), lambda i: (0, 0))],
        out_specs=[pl.BlockSpec((rows, half), lambda i: (i, 0))] * 2,
        out_shape=[out, out],
        compiler_params=_compiler_params(1, 3 * _nbytes((rows, LANES), F32),
                                         4 * _nbytes((rows, LANES), F32)),
        name="rope_tables",
    )(positions.reshape(m, 1), inv_freq.reshape(1, half))


def _retention_kernel(dec_ref, q_ref, k_ref, v_ref, g_ref, cos_ref, sin_ref, mask_ref,
                      qd_ref, kd_ref, o_ref, state_ref, *, n_sub, scale):
    head = pl.program_id(1)

    @pl.when(pl.program_id(2) == 0)
    def _():
        state_ref[...] = jnp.zeros_like(state_ref)

    dec = dec_ref[head]
    mask = mask_ref[0]
    qd = qd_ref[0]
    kd = kd_ref[0]
    sub = mask.shape[0]
    half = q_ref.shape[-1] // 2

    def rope(t, cos, sin):
        t1, t2 = t[:, :half], t[:, half:]
        return jnp.concatenate([t1 * cos - t2 * sin, t2 * cos + t1 * sin], axis=-1)

    for s in range(n_sub):
        rows = pl.ds(s * sub, sub)
        cos = cos_ref[rows, :]
        sin = sin_ref[rows, :]
        q = rope(q_ref[rows, :].astype(F32), cos, sin)
        k = rope(k_ref[rows, :].astype(F32), cos, sin) * scale
        v = v_ref[rows, :]
        qb = q.astype(MXU_DTYPE)
        kb = k.astype(MXU_DTYPE)
        scores = lax.dot_general(qb, kb, (((1,), (1,)), ((), ())),
                                 preferred_element_type=F32) * mask
        state = state_ref[...]
        o = (jnp.dot(scores.astype(MXU_DTYPE), v, preferred_element_type=F32)
             + jnp.dot(qb, state.astype(MXU_DTYPE), preferred_element_type=F32) * qd)
        kdb = (k * kd).astype(MXU_DTYPE)
        state_ref[...] = state * dec + lax.dot_general(
            kdb, v, (((0,), (0,)), ((), ())), preferred_element_type=F32)
        ms = jnp.mean(o * o, axis=-1, keepdims=True)
        of = o * lax.rsqrt(ms + EPS)
        g = g_ref[rows, :].astype(F32)
        o_ref[rows, :] = (of * (g * jax.nn.sigmoid(g))).astype(o_ref.dtype)


def _retention(proj, cos, sin, batch, seq, d_model):
    m = proj.shape[0]
    heads = RET_HEADS
    dk = d_model // heads
    dv = (proj.shape[1] - 2 * d_model) // (2 * heads)
    sub = _pick_tile(seq, RET_SUB, CHUNK)
    tt = _pick_tile(seq, 1024, sub)
    n_sub = tt // sub
    tpb = seq // tt
    v_blk0 = 2 * d_model // dv

    log_g = jnp.log1p(-jnp.exp2(-5.0 - jnp.arange(heads, dtype=F32)))
    idx = jnp.arange(sub, dtype=F32)
    dist = jnp.abs(idx[:, None] - idx[None, :])
    chunk_id = jnp.arange(sub) // CHUNK
    visible = chunk_id[None, :] <= chunk_id[:, None]
    mask = jnp.where(visible[None], jnp.exp(log_g[:, None, None] * dist[None]), 0.0)
    q_dec = jnp.exp(log_g[:, None] * (idx + 1.0))[:, :, None]
    k_dec = jnp.exp(log_g[:, None] * (sub - 1.0 - idx))[:, :, None]
    blk_dec = jnp.exp(log_g * sub)

    row = lambda b, h, t: b * tpb + t
    blocks = (2 * _nbytes((tt, dk), proj.dtype) + 2 * _nbytes((tt, dv), proj.dtype)
              + 2 * _nbytes((tt, dk // 2), F32)
              + _nbytes((sub, sub), F32) + 2 * _nbytes((sub, LANES), F32)
              + _nbytes((tt, dv), MXU_DTYPE))
    temps = _nbytes((dk, dv), F32) + 8 * _nbytes((sub, dv), F32)
    return pl.pallas_call(
        functools.partial(_retention_kernel, n_sub=n_sub, scale=dk ** -0.5),
        grid=(batch, heads, tpb),
        in_specs=[
            pl.BlockSpec(memory_space=pltpu.SMEM),
            pl.BlockSpec((tt, dk), lambda b, h, t: (row(b, h, t), h)),
            pl.BlockSpec((tt, dk), lambda b, h, t: (row(b, h, t), heads + h)),
            pl.BlockSpec((tt, dv), lambda b, h, t: (row(b, h, t), v_blk0 + h)),
            pl.BlockSpec((tt, dv), lambda b, h, t: (row(b, h, t), v_blk0 + heads + h)),
            pl.BlockSpec((tt, dk // 2), lambda b, h, t: (row(b, h, t), 0)),
            pl.BlockSpec((tt, dk // 2), lambda b, h, t: (row(b, h, t), 0)),
            pl.BlockSpec((1, sub, sub), lambda b, h, t: (h, 0, 0)),
            pl.BlockSpec((1, sub, 1), lambda b, h, t: (h, 0, 0)),
            pl.BlockSpec((1, sub, 1), lambda b, h, t: (h, 0, 0)),
        ],
        out_specs=pl.BlockSpec((tt, dv), lambda b, h, t: (row(b, h, t), h)),
        out_shape=jax.ShapeDtypeStruct((m, heads * dv), MXU_DTYPE),
        scratch_shapes=[pltpu.VMEM((dk, dv), F32)],
        compiler_params=_compiler_params(3, blocks, temps),
        name="retention",
    )(blk_dec, proj, proj, proj, proj, cos, sin, mask, q_dec, k_dec)


_P_CONV_B, _P_BIAS_R, _P_BIAS_I, _P_LAMBDA = CONV_WIDTH, CONV_WIDTH + 1, CONV_WIDTH + 2, CONV_WIDTH + 3


def _gate_windows(blk, group):
    width = min(group, _round_up(blk, LANES) + LANES)
    starts = tuple(min(n * blk // LANES * LANES, group - width) for n in range(group // blk))
    return width, starts


def _gelu_tanh(x):
    c = 0.7978845608028654
    return x * (0.5 * (1.0 + jnp.tanh(c * (x + 0.044715 * (x * x * x)))))


def _lru_kernel(x_ref, gb_ref, w_ref, p_ref, o_ref, xs_ref, gs_ref, xc_ref, xb_ref, gates_ref,
                os_ref, h_ref, *, starts, width):
    tt, gw = x_ref.shape
    hist = SUBLANES
    n_tiles = tt // SUBLANES
    n_slabs = gw // LANES

    def phase(ref, slab, first_row):
        return ref[slab, pl.ds(first_row, n_tiles, stride=SUBLANES), :]

    @pl.when(pl.program_id(2) == 0)
    def _():
        h_ref[...] = jnp.zeros_like(h_ref)
        xs_ref[:, 0:hist, :] = jnp.zeros((n_slabs, hist, LANES), F32)

    for s in range(n_slabs):
        lanes = slice(s * LANES, (s + 1) * LANES)
        xs_ref[s, hist:hist + tt, :] = x_ref[:, lanes]
        gs_ref[s] = gb_ref[:, lanes]
    p = p_ref[...]

    for s in range(n_slabs):
        lanes = slice(s * LANES, (s + 1) * LANES)
        taps = {o: phase(xs_ref, s, o) for o in range(hist - (CONV_WIDTH - 1), hist + SUBLANES)}
        for j in range(SUBLANES):
            xc = p[_P_CONV_B:_P_CONV_B + 1, lanes]
            for k in range(CONV_WIDTH):
                xc = xc + p[k:k + 1, lanes] * taps[hist + j - (CONV_WIDTH - 1) + k]
            xc_ref[s, j] = xc
            xb_ref[j * n_tiles:(j + 1) * n_tiles, lanes] = xc.astype(MXU_DTYPE)

    gates_ref[...] = jnp.zeros_like(gates_ref)
    for n, ws in enumerate(starts):
        win = slice(ws, ws + width)
        g = jnp.dot(xb_ref[:, win], w_ref[n], preferred_element_type=F32)
        gates_ref[:, win] += g[:, :width]
        gates_ref[:, gw + ws:gw + ws + width] += g[:, width:]

    tile = lax.broadcasted_iota(jnp.int32, (n_tiles, LANES), 0)
    for s in range(n_slabs):
        lanes = slice(s * LANES, (s + 1) * LANES)
        lanes_i = slice(gw + s * LANES, gw + (s + 1) * LANES)
        bias_r = p[_P_BIAS_R:_P_BIAS_R + 1, lanes]
        bias_i = p[_P_BIAS_I:_P_BIAS_I + 1, lanes]
        neg_lam = -p[_P_LAMBDA:_P_LAMBDA + 1, lanes]
        softplus = jnp.maximum(neg_lam, 0.0) + jnp.log1p(jnp.exp(-jnp.abs(neg_lam)))

        h_loc, a_cum = [], []
        for j in range(SUBLANES):
            rows = slice(j * n_tiles, (j + 1) * n_tiles)
            xc = xc_ref[s, j]
            r = jax.nn.sigmoid(gates_ref[rows, lanes] + bias_r)
            i = jax.nn.sigmoid(gates_ref[rows, lanes_i] + bias_i)
            log_a = (-LRU_C * r) * softplus
            a = jnp.exp(log_a)
            y = 1.0 - a * a
            u = (y * lax.rsqrt(jnp.maximum(y, SQRT_FLOOR))) * (i * xc)
            h_loc.append(u if j == 0 else a * h_loc[-1] + u)
            a_cum.append(a if j == 0 else a * a_cum[-1])

        a_inc, u_inc = a_cum[-1], h_loc[-1]
        d = 1
        while d < n_tiles:
            keep = tile >= d
            a_prev = jnp.where(keep, pltpu.roll(a_inc, d, 0), 1.0)
            u_prev = jnp.where(keep, pltpu.roll(u_inc, d, 0), 0.0)
            u_inc = u_inc + a_inc * u_prev
            a_inc = a_inc * a_prev
            d *= 2
        h_prev = jnp.broadcast_to(h_ref[s], (n_tiles, LANES))
        h_end = u_inc + a_inc * h_prev
        h_in = jnp.where(tile >= 1, pltpu.roll(h_end, 1, 0), h_prev)
        h_ref[s] = h_end[n_tiles - 1:n_tiles, :]

        for j in range(SUBLANES):
            hs = h_loc[j] + a_cum[j] * h_in
            os_ref[s, pl.ds(j, n_tiles, stride=SUBLANES), :] = hs * _gelu_tanh(phase(gs_ref, s, j))
        o_ref[:, lanes] = os_ref[s].astype(o_ref.dtype)

    xs_ref[:, 0:hist, :] = xs_ref[:, tt:tt + hist, :]


def _lru(proj, w_win, params, batch, seq, group, width, starts):
    m = proj.shape[0]
    lru_width = proj.shape[1] // 2
    n_groups = lru_width // group
    n_slabs = group // LANES
    bpg = len(starts)
    tt = _pick_tile(seq, LRU_ROWS, SUBLANES * SUBLANES)
    assert tt % (SUBLANES * SUBLANES) == 0
    tpb = seq // tt
    row = lambda g, b, t: b * tpb + t
    blocks = (2 * _nbytes((tt, group), proj.dtype) + _nbytes((SUBLANES, group), F32)
              + _nbytes((tt, group), MXU_DTYPE))
    scratch = [pltpu.VMEM((n_slabs, tt + SUBLANES, LANES), F32),
               pltpu.VMEM((n_slabs, tt, LANES), F32),
               pltpu.VMEM((n_slabs, SUBLANES, tt // SUBLANES, LANES), F32),
               pltpu.VMEM((tt, group), MXU_DTYPE),
               pltpu.VMEM((tt, 2 * group), F32),
               pltpu.VMEM((n_slabs, tt, LANES), F32),
               pltpu.VMEM((n_slabs, 1, LANES), F32)]
    resident = (_nbytes((bpg, width, 2 * width), MXU_DTYPE) + 4 * _nbytes((tt + SUBLANES, group), F32)
                + _nbytes((tt, group), MXU_DTYPE) + _nbytes((tt, 2 * group), F32)
                + 2 * _nbytes((tt, 2 * width), F32))
    return pl.pallas_call(
        functools.partial(_lru_kernel, starts=starts, width=width),
        grid=(n_groups, batch, tpb),
        in_specs=[
            pl.BlockSpec((tt, group), lambda g, b, t: (row(g, b, t), g)),
            pl.BlockSpec((tt, group), lambda g, b, t: (row(g, b, t), n_groups + g)),
            pl.BlockSpec((bpg, width, 2 * width), lambda g, b, t: (g, 0, 0),
                         pipeline_mode=pl.Buffered(1)),
            pl.BlockSpec((SUBLANES, group), lambda g, b, t: (0, g)),
        ],
        out_specs=pl.BlockSpec((tt, group), lambda g, b, t: (row(g, b, t), g)),
        out_shape=jax.ShapeDtypeStruct((m, lru_width), MXU_DTYPE),
        scratch_shapes=scratch,
        compiler_params=_compiler_params(3, blocks, resident),
        name="rglru",
    )(proj, proj, w_win, params)


def _embed_gate_weights(w_r, w_i, group, width, starts):
    blk = w_r.shape[1]
    bpg = len(starts)
    out = []
    for n in range(w_r.shape[0]):
        off = (n % bpg) * blk - starts[n % bpg]
        pad = ((off, width - blk - off), (off, width - blk - off))
        out.append(jnp.concatenate([jnp.pad(w_r[n], pad), jnp.pad(w_i[n], pad)], axis=1))
    return jnp.stack(out).astype(MXU_DTYPE)


_NORM_BK = 4096


def _mlp(h, hg, ss, w_up, w_down, layer, next_gain):
    z = _matmul(hg, w_up, layer, MXU_DTYPE, epilogue="relu2", row_sumsq=ss)
    if next_gain is None:
        return _matmul(z, w_down, layer, F32, epilogue="residual", res=h), None, None
    return _matmul(z, w_down, layer, F32, epilogue="residual", res=h, next_gain=next_gain,
                   bk=_NORM_BK)


def _retention_layer(h, hg, ss, positions, w_in, w_out, layer, next_gain, batch, seq):
    d = h.shape[1]
    proj = _matmul(hg, w_in, layer, MXU_DTYPE, row_sumsq=ss)
    cos, sin = _rope_tables(positions, d // RET_HEADS // 2)
    o = _retention(proj, cos, sin, batch, seq, d)
    return _matmul(o, w_out, layer, F32, epilogue="residual", res=h, next_gain=next_gain,
                   bk=_NORM_BK)


def _lru_layer(h, hg, ss, w_in, conv_w, conv_b, w_rgate, b_rgate, w_igate, b_igate, lam, w_out,
               layer, next_gain, batch, seq):
    lru_width = w_out.shape[1]
    blk = lru_width // LRU_BLOCKS
    group = blk * LANES // math.gcd(blk, LANES)
    width, starts = _gate_windows(blk, group)
    w_win = _embed_gate_weights(w_rgate, w_igate, group, width, starts)
    params = jnp.concatenate(
        [conv_w] + [v.reshape(1, lru_width) for v in (conv_b, b_rgate, b_igate, lam)],
        axis=0).astype(F32)
    proj = _matmul(hg, w_in, layer, F32, row_sumsq=ss)
    mixed = _lru(proj, w_win, params, batch, seq, group, width, starts)
    return _matmul(mixed, w_out, layer, F32, epilogue="residual", res=h, next_gain=next_gain,
                   bk=_NORM_BK)


def kernel(x, positions, norm_mix_g, norm_mlp_g, final_norm_g, ret_w_in, ret_w_out,
           lru_w_in, lru_conv_w, lru_conv_b, lru_w_rgate, lru_b_rgate, lru_w_igate,
           lru_b_igate, lru_lambda, lru_w_out, mlp_w_up, mlp_w_down):
    batch, seq, d = x.shape
    h = x.reshape(batch * seq, d)
    depth = norm_mix_g.shape[0]
    n_mixers = 2
    cast = lambda w: w.astype(MXU_DTYPE)
    ret_w_in, ret_w_out, lru_w_in, lru_w_out, mlp_w_up, mlp_w_down = map(
        cast, (ret_w_in, ret_w_out, lru_w_in, lru_w_out, mlp_w_up, mlp_w_down))
    hg, ss = _norm_operand(h, norm_mix_g[0])
    for i in range(depth):
        j = i // n_mixers
        if i % n_mixers == 0:
            h, hg, ss = _retention_layer(h, hg, ss, positions, ret_w_in, ret_w_out, j,
                                         norm_mlp_g[i], batch, seq)
        else:
            h, hg, ss = _lru_layer(h, hg, ss, lru_w_in, lru_conv_w[j], lru_conv_b[j],
                                   lru_w_rgate[j], lru_b_rgate[j], lru_w_igate[j],
                                   lru_b_igate[j], lru_lambda[j], lru_w_out, j,
                                   norm_mlp_g[i], batch, seq)
        next_gain = norm_mix_g[i + 1] if i + 1 < depth else None
        h, hg, ss = _mlp(h, hg, ss, mlp_w_up, mlp_w_down, i, next_gain)
    return _rmsnorm(h, final_norm_g, x.dtype).reshape(batch, seq, d)
```

```python
import functools
import math

import jax
import jax.numpy as jnp
from jax import lax
from jax.experimental import pallas as pl
from jax.experimental.pallas import tpu as pltpu

RET_HEADS = 16
CHUNK = 64
ROPE_THETA = 10000.0
LRU_BLOCKS = 16
CONV_WIDTH = 4
LRU_C = 8.0
EPS = 1e-6

LANES = 128
SUBLANES = 8
MXU_WIDTH = 256
VMEM_BYTES_V7X = 64 * 1024 * 1024
VMEM_COMPILER_RESERVE = 2 * 1024 * 1024
VMEM_LIMIT_FLOOR = 16 * 1024 * 1024

MXU_DTYPE = jnp.bfloat16
F32 = jnp.float32

RET_SUB = 256
LRU_ROWS = 256
SQRT_FLOOR = 1e-30


def _round_up(x, m):
    return (x + m - 1) // m * m


def _pick_tile(dim, target, align):
    if dim <= target:
        return dim
    t = target // align * align
    while t >= align:
        if dim % t == 0:
            return t
        t -= align
    return dim


def _nbytes(shape, dtype):
    n = 1
    for s in shape:
        n *= s
    return n * jnp.dtype(dtype).itemsize


def _compiler_params(n_axes, pipelined_bytes, resident_bytes):
    limit = 2 * pipelined_bytes + resident_bytes
    limit = min(max(limit, VMEM_LIMIT_FLOOR), VMEM_BYTES_V7X - VMEM_COMPILER_RESERVE)
    return pltpu.CompilerParams(
        dimension_semantics=("arbitrary",) * n_axes, vmem_limit_bytes=int(limit))


def _rmsnorm_kernel(x_ref, g_ref, o_ref):
    x = x_ref[...]
    ms = jnp.mean(x * x, axis=-1, keepdims=True)
    o_ref[...] = ((x * lax.rsqrt(ms + EPS)) * g_ref[...]).astype(o_ref.dtype)


def _rmsnorm(x, g, out_dtype):
    m, d = x.shape
    rows = _pick_tile(m, 256, SUBLANES)
    blocks = _nbytes((rows, d), x.dtype) + _nbytes((rows, d), out_dtype)
    return pl.pallas_call(
        _rmsnorm_kernel,
        grid=(m // rows,),
        in_specs=[pl.BlockSpec((rows, d), lambda i: (i, 0)),
                  pl.BlockSpec((1, d), lambda i: (0, 0))],
        out_specs=pl.BlockSpec((rows, d), lambda i: (i, 0)),
        out_shape=jax.ShapeDtypeStruct((m, d), out_dtype),
        compiler_params=_compiler_params(1, blocks, 2 * _nbytes((rows, d), F32)),
        name="rmsnorm",
    )(x, g.reshape(1, d).astype(F32))


def _row_sumsq(x):
    return jnp.broadcast_to(jnp.sum(x * x, axis=-1, keepdims=True), (x.shape[0], LANES))


def _norm_operand_kernel(x_ref, g_ref, xg_ref, ss_ref):
    x = x_ref[...]
    xg_ref[...] = (x * g_ref[...]).astype(xg_ref.dtype)
    ss_ref[...] = _row_sumsq(x)


def _norm_operand(x, g):
    m, d = x.shape
    rows = _pick_tile(m, 256, SUBLANES)
    blocks = (_nbytes((rows, d), x.dtype) + _nbytes((rows, d), MXU_DTYPE)
              + _nbytes((rows, LANES), F32))
    return pl.pallas_call(
        _norm_operand_kernel,
        grid=(m // rows,),
        in_specs=[pl.BlockSpec((rows, d), lambda i: (i, 0)),
                  pl.BlockSpec((1, d), lambda i: (0, 0))],
        out_specs=[pl.BlockSpec((rows, d), lambda i: (i, 0)),
                   pl.BlockSpec((rows, LANES), lambda i: (i, 0))],
        out_shape=[jax.ShapeDtypeStruct((m, d), MXU_DTYPE),
                   jax.ShapeDtypeStruct((m, LANES), F32)],
        compiler_params=_compiler_params(1, blocks, 2 * _nbytes((rows, d), F32)),
        name="norm_operand",
    )(x, g.reshape(1, d).astype(F32))


def _matmul_kernel(*refs, nk, epilogue, scaled, emit_norm, inv_dim):
    refs = iter(refs)
    x_ref, w_ref = next(refs), next(refs)
    ss_ref = next(refs) if scaled else None
    r_ref = next(refs) if epilogue == "residual" else None
    g_ref = next(refs) if emit_norm else None
    o_ref = next(refs)
    og_ref, oss_ref = (next(refs), next(refs)) if emit_norm else (None, None)

    def product():
        acc = jnp.dot(x_ref[...], w_ref[...].astype(MXU_DTYPE), preferred_element_type=F32)
        if scaled:
            ss = ss_ref[...]
            total = ss[:, :LANES]
            for j in range(1, ss.shape[1] // LANES):
                total = total + ss[:, j * LANES:(j + 1) * LANES]
            scale = lax.rsqrt(total * inv_dim + EPS)
            acc = acc * jnp.tile(scale, (1, acc.shape[1] // LANES))
        return acc

    def store(out):
        o_ref[...] = out.astype(o_ref.dtype)
        if emit_norm:
            og_ref[...] = (out * g_ref[...]).astype(og_ref.dtype)
            oss_ref[...] = _row_sumsq(out)

    def first(last):
        acc = product()
        if epilogue == "relu2":
            z = jnp.maximum(acc, 0.0)
            acc = z * z
        elif epilogue == "residual":
            acc = r_ref[...] + acc
        if last:
            store(acc)
        else:
            o_ref[...] = acc

    if nk == 1:
        first(True)
        return

    k = pl.program_id(2)
    pl.when(k == 0)(functools.partial(first, False))

    if not emit_norm:
        @pl.when(k > 0)
        def _():
            o_ref[...] += product()
        return

    if nk > 2:
        @pl.when(jnp.logical_and(k > 0, k < nk - 1))
        def _():
            o_ref[...] += product()

    @pl.when(k == nk - 1)
    def _():
        store(o_ref[...] + product())


def _matmul(x, w, layer, out_dtype, epilogue="none", res=None, row_sumsq=None, next_gain=None,
            bm=1024, bn=1024, bk=4096):
    m, kdim = x.shape
    n = w.shape[2]
    bm = _pick_tile(m, bm, SUBLANES)
    bn = _pick_tile(n, bn, MXU_WIDTH if n % MXU_WIDTH == 0 else LANES)
    bk = _pick_tile(kdim, bk, LANES)
    nk = kdim // bk
    scaled = row_sumsq is not None
    emit_norm = next_gain is not None
    assert nk == 1 or (out_dtype == F32 and epilogue != "relu2" and not scaled)
    cast_w = w.dtype != MXU_DTYPE
    assert nk == 1 or not cast_w
    row_mode = dict(pipeline_mode=pl.Buffered(1)) if cast_w else {}
    in_specs = [pl.BlockSpec((bm, bk), lambda i, j, k: (i, k), **row_mode),
                pl.BlockSpec((None, bk, bn), lambda i, j, k: (layer, k, j))]
    args = [x, w]
    blocks = _nbytes((bk, bn), w.dtype) + _nbytes((bm, bn), out_dtype)
    row_bytes = _nbytes((bm, bk), x.dtype)
    if scaled:
        in_specs.append(pl.BlockSpec((bm, row_sumsq.shape[1]), lambda i, j, k: (i, 0), **row_mode))
        args.append(row_sumsq)
        row_bytes += _nbytes((bm, row_sumsq.shape[1]), F32)
    temps = 3 * _nbytes((bm, bn), F32)
    if cast_w:
        temps += row_bytes + _nbytes((bk, bn), MXU_DTYPE)
    else:
        blocks += row_bytes
    if epilogue == "residual":
        in_specs.append(pl.BlockSpec((bm, bn), lambda i, j, k: (i, j)))
        args.append(res)
        blocks += _nbytes((bm, bn), res.dtype)
    out_specs = pl.BlockSpec((bm, bn), lambda i, j, k: (i, j))
    out_shape = jax.ShapeDtypeStruct((m, n), out_dtype)
    if emit_norm:
        in_specs.append(pl.BlockSpec((1, bn), lambda i, j, k: (0, j)))
        args.append(next_gain.reshape(1, n).astype(F32))
        out_specs = [out_specs, pl.BlockSpec((bm, bn), lambda i, j, k: (i, j)),
                     pl.BlockSpec((bm, LANES), lambda i, j, k: (i, j))]
        out_shape = [out_shape, jax.ShapeDtypeStruct((m, n), MXU_DTYPE),
                     jax.ShapeDtypeStruct((m, n // bn * LANES), F32)]
        blocks += _nbytes((bm, bn), MXU_DTYPE) + _nbytes((bm, LANES), F32)
    return pl.pallas_call(
        functools.partial(_matmul_kernel, nk=nk, epilogue=epilogue, scaled=scaled,
                          emit_norm=emit_norm, inv_dim=1.0 / kdim),
        grid=(m // bm, n // bn, nk),
        in_specs=in_specs,
        out_specs=out_specs,
        out_shape=out_shape,
        compiler_params=_compiler_params(3, blocks, temps),
        name="matmul_" + epilogue,
    )(*args)


def _rope_table_kernel(pos_ref, freq_ref, cos_ref, sin_ref):
    ang = pos_ref[...].astype(F32) * freq_ref[...]
    cos_ref[...] = jnp.cos(ang)
    sin_ref[...] = jnp.sin(ang)


def _rope_tables(positions, half):
    m = positions.size
    rows = _pick_tile(m, 1024, SUBLANES)
    inv_freq = ROPE_THETA ** (-jnp.arange(half, dtype=F32) / half)
    out = jax.ShapeDtypeStruct((m, half), F32)
    return pl.pallas_call(
        _rope_table_kernel,
        grid=(m // rows,),
        in_specs=[pl.BlockSpec((rows, 1), lambda i: (i, 0)),
                  pl.BlockSpec((1, half), lambda i: (0, 0))],
        out_specs=[pl.BlockSpec((rows, half), lambda i: (i, 0))] * 2,
        out_shape=[out, out],
        compiler_params=_compiler_params(1, 3 * _nbytes((rows, LANES), F32),
                                         4 * _nbytes((rows, LANES), F32)),
        name="rope_tables",
    )(positions.reshape(m, 1), inv_freq.reshape(1, half))


def _retention_kernel(dec_ref, q_ref, k_ref, v_ref, g_ref, cos_ref, sin_ref, mask_ref,
                      qd_ref, kd_ref, o_ref, state_ref, *, n_sub, scale):
    head = pl.program_id(1)

    @pl.when(pl.program_id(2) == 0)
    def _():
        state_ref[...] = jnp.zeros_like(state_ref)

    dec = dec_ref[head]
    mask = mask_ref[0]
    qd = qd_ref[0]
    kd = kd_ref[0]
    sub = mask.shape[0]
    half = q_ref.shape[-1] // 2

    def rope(t, cos, sin):
        t1, t2 = t[:, :half], t[:, half:]
        return jnp.concatenate([t1 * cos - t2 * sin, t2 * cos + t1 * sin], axis=-1)

    for s in range(n_sub):
        rows = pl.ds(s * sub, sub)
        cos = cos_ref[rows, :]
        sin = sin_ref[rows, :]
        q = rope(q_ref[rows, :].astype(F32), cos, sin)
        k = rope(k_ref[rows, :].astype(F32), cos, sin) * scale
        v = v_ref[rows, :]
        qb = q.astype(MXU_DTYPE)
        kb = k.astype(MXU_DTYPE)
        scores = lax.dot_general(qb, kb, (((1,), (1,)), ((), ())),
                                 preferred_element_type=F32) * mask
        state = state_ref[...]
        o = (jnp.dot(scores.astype(MXU_DTYPE), v, preferred_element_type=F32)
             + jnp.dot(qb, state.astype(MXU_DTYPE), preferred_element_type=F32) * qd)
        kdb = (k * kd).astype(MXU_DTYPE)
        state_ref[...] = state * dec + lax.dot_general(
            kdb, v, (((0,), (0,)), ((), ())), preferred_element_type=F32)
        ms = jnp.mean(o * o, axis=-1, keepdims=True)
        of = o * lax.rsqrt(ms + EPS)
        g = g_ref[rows, :].astype(F32)
        o_ref[rows, :] = (of * (g * jax.nn.sigmoid(g))).astype(o_ref.dtype)


def _retention(proj, cos, sin, batch, seq, d_model):
    m = proj.shape[0]
    heads = RET_HEADS
    dk = d_model // heads
    dv = (proj.shape[1] - 2 * d_model) // (2 * heads)
    sub = _pick_tile(seq, RET_SUB, CHUNK)
    tt = _pick_tile(seq, 1024, sub)
    n_sub = tt // sub
    tpb = seq // tt
    v_blk0 = 2 * d_model // dv

    log_g = jnp.log1p(-jnp.exp2(-5.0 - jnp.arange(heads, dtype=F32)))
    idx = jnp.arange(sub, dtype=F32)
    dist = jnp.abs(idx[:, None] - idx[None, :])
    chunk_id = jnp.arange(sub) // CHUNK
    visible = chunk_id[None, :] <= chunk_id[:, None]
    mask = jnp.where(visible[None], jnp.exp(log_g[:, None, None] * dist[None]), 0.0)
    q_dec = jnp.exp(log_g[:, None] * (idx + 1.0))[:, :, None]
    k_dec = jnp.exp(log_g[:, None] * (sub - 1.0 - idx))[:, :, None]
    blk_dec = jnp.exp(log_g * sub)

    row = lambda b, h, t: b * tpb + t
    blocks = (2 * _nbytes((tt, dk), proj.dtype) + 2 * _nbytes((tt, dv), proj.dtype)
              + 2 * _nbytes((tt, dk // 2), F32)
              + _nbytes((sub, sub), F32) + 2 * _nbytes((sub, LANES), F32)
              + _nbytes((tt, dv), MXU_DTYPE))
    temps = _nbytes((dk, dv), F32) + 8 * _nbytes((sub, dv), F32)
    return pl.pallas_call(
        functools.partial(_retention_kernel, n_sub=n_sub, scale=dk ** -0.5),
        grid=(batch, heads, tpb),
        in_specs=[
            pl.BlockSpec(memory_space=pltpu.SMEM),
            pl.BlockSpec((tt, dk), lambda b, h, t: (row(b, h, t), h)),
            pl.BlockSpec((tt, dk), lambda b, h, t: (row(b, h, t), heads + h)),
            pl.BlockSpec((tt, dv), lambda b, h, t: (row(b, h, t), v_blk0 + h)),
            pl.BlockSpec((tt, dv), lambda b, h, t: (row(b, h, t), v_blk0 + heads + h)),
            pl.BlockSpec((tt, dk // 2), lambda b, h, t: (row(b, h, t), 0)),
            pl.BlockSpec((tt, dk // 2), lambda b, h, t: (row(b, h, t), 0)),
            pl.BlockSpec((1, sub, sub), lambda b, h, t: (h, 0, 0)),
            pl.BlockSpec((1, sub, 1), lambda b, h, t: (h, 0, 0)),
            pl.BlockSpec((1, sub, 1), lambda b, h, t: (h, 0, 0)),
        ],
        out_specs=pl.BlockSpec((tt, dv), lambda b, h, t: (row(b, h, t), h)),
        out_shape=jax.ShapeDtypeStruct((m, heads * dv), MXU_DTYPE),
        scratch_shapes=[pltpu.VMEM((dk, dv), F32)],
        compiler_params=_compiler_params(3, blocks, temps),
        name="retention",
    )(blk_dec, proj, proj, proj, proj, cos, sin, mask, q_dec, k_dec)


_P_CONV_B, _P_BIAS_R, _P_BIAS_I, _P_LAMBDA = CONV_WIDTH, CONV_WIDTH + 1, CONV_WIDTH + 2, CONV_WIDTH + 3


def _gate_windows(blk, group):
    width = min(group, _round_up(blk, LANES) + LANES)
    starts = tuple(min(n * blk // LANES * LANES, group - width) for n in range(group // blk))
    return width, starts


def _gelu_tanh(x):
    c = 0.7978845608028654
    return x * (0.5 * (1.0 + jnp.tanh(c * (x + 0.044715 * (x * x * x)))))


def _lru_kernel(x_ref, gb_ref, w_ref, p_ref, o_ref, xs_ref, gs_ref, xc_ref, xb_ref, gates_ref,
                os_ref, h_ref, *, starts, width):
    tt, gw = x_ref.shape
    hist = SUBLANES
    n_tiles = tt // SUBLANES
    n_slabs = gw // LANES

    def phase(ref, slab, first_row):
        return ref[slab, pl.ds(first_row, n_tiles, stride=SUBLANES), :]

    @pl.when(pl.program_id(2) == 0)
    def _():
        h_ref[...] = jnp.zeros_like(h_ref)
        xs_ref[:, 0:hist, :] = jnp.zeros((n_slabs, hist, LANES), F32)

    for s in range(n_slabs):
        lanes = slice(s * LANES, (s + 1) * LANES)
        xs_ref[s, hist:hist + tt, :] = x_ref[:, lanes]
        gs_ref[s] = gb_ref[:, lanes]
    p = p_ref[...]

    for s in range(n_slabs):
        lanes = slice(s * LANES, (s + 1) * LANES)
        taps = {o: phase(xs_ref, s, o) for o in range(hist - (CONV_WIDTH - 1), hist + SUBLANES)}
        for j in range(SUBLANES):
            xc = p[_P_CONV_B:_P_CONV_B + 1, lanes]
            for k in range(CONV_WIDTH):
                xc = xc + p[k:k + 1, lanes] * taps[hist + j - (CONV_WIDTH - 1) + k]
            xc_ref[s, j] = xc
            xb_ref[j * n_tiles:(j + 1) * n_tiles, lanes] = xc.astype(MXU_DTYPE)

    written = set()
    for n, ws in enumerate(starts):
        g = jnp.dot(xb_ref[:, ws:ws + width], w_ref[n], preferred_element_type=F32)
        for c in range(0, width, LANES):
            for src, dst in ((c, ws + c), (width + c, gw + ws + c)):
                if dst in written:
                    gates_ref[:, dst:dst + LANES] += g[:, src:src + LANES]
                else:
                    gates_ref[:, dst:dst + LANES] = g[:, src:src + LANES]
                    written.add(dst)
    assert len(written) == 2 * n_slabs

    tile = lax.broadcasted_iota(jnp.int32, (n_tiles, LANES), 0)
    for s in range(n_slabs):
        lanes = slice(s * LANES, (s + 1) * LANES)
        lanes_i = slice(gw + s * LANES, gw + (s + 1) * LANES)
        bias_r = p[_P_BIAS_R:_P_BIAS_R + 1, lanes]
        bias_i = p[_P_BIAS_I:_P_BIAS_I + 1, lanes]
        neg_lam = -p[_P_LAMBDA:_P_LAMBDA + 1, lanes]
        softplus = jnp.maximum(neg_lam, 0.0) + jnp.log1p(jnp.exp(-jnp.abs(neg_lam)))

        h_loc, a_cum = [], []
        for j in range(SUBLANES):
            rows = slice(j * n_tiles, (j + 1) * n_tiles)
            xc = xc_ref[s, j]
            r = jax.nn.sigmoid(gates_ref[rows, lanes] + bias_r)
            i = jax.nn.sigmoid(gates_ref[rows, lanes_i] + bias_i)
            log_a = (-LRU_C * r) * softplus
            a = jnp.exp(log_a)
            y = 1.0 - a * a
            u = (y * lax.rsqrt(jnp.maximum(y, SQRT_FLOOR))) * (i * xc)
            h_loc.append(u if j == 0 else a * h_loc[-1] + u)
            a_cum.append(a if j == 0 else a * a_cum[-1])

        a_inc, u_inc = a_cum[-1], h_loc[-1]
        d = 1
        while d < n_tiles:
            keep = tile >= d
            a_prev = jnp.where(keep, pltpu.roll(a_inc, d, 0), 1.0)
            u_prev = jnp.where(keep, pltpu.roll(u_inc, d, 0), 0.0)
            u_inc = u_inc + a_inc * u_prev
            a_inc = a_inc * a_prev
            d *= 2
        h_prev = jnp.broadcast_to(h_ref[s], (n_tiles, LANES))
        h_end = u_inc + a_inc * h_prev
        h_in = jnp.where(tile >= 1, pltpu.roll(h_end, 1, 0), h_prev)
        h_ref[s] = h_end[n_tiles - 1:n_tiles, :]

        for j in range(SUBLANES):
            hs = h_loc[j] + a_cum[j] * h_in
            os_ref[s, pl.ds(j, n_tiles, stride=SUBLANES), :] = hs * _gelu_tanh(phase(gs_ref, s, j))
        o_ref[:, lanes] = os_ref[s].astype(o_ref.dtype)

    xs_ref[:, 0:hist, :] = xs_ref[:, tt:tt + hist, :]


def _lru(proj, w_win, params, batch, seq, group, width, starts):
    m = proj.shape[0]
    lru_width = proj.shape[1] // 2
    n_groups = lru_width // group
    n_slabs = group // LANES
    bpg = len(starts)
    tt = _pick_tile(seq, LRU_ROWS, SUBLANES * SUBLANES)
    assert tt % (SUBLANES * SUBLANES) == 0
    tpb = seq // tt
    row = lambda g, b, t: b * tpb + t
    blocks = (2 * _nbytes((tt, group), proj.dtype) + _nbytes((SUBLANES, group), F32)
              + _nbytes((tt, group), MXU_DTYPE))
    scratch = [pltpu.VMEM((n_slabs, tt + SUBLANES, LANES), F32),
               pltpu.VMEM((n_slabs, tt, LANES), F32),
               pltpu.VMEM((n_slabs, SUBLANES, tt // SUBLANES, LANES), F32),
               pltpu.VMEM((tt, group), MXU_DTYPE),
               pltpu.VMEM((tt, 2 * group), F32),
               pltpu.VMEM((n_slabs, tt, LANES), F32),
               pltpu.VMEM((n_slabs, 1, LANES), F32)]
    resident = (_nbytes((bpg, width, 2 * width), MXU_DTYPE) + 4 * _nbytes((tt + SUBLANES, group), F32)
                + _nbytes((tt, group), MXU_DTYPE) + _nbytes((tt, 2 * group), F32)
                + 2 * _nbytes((tt, 2 * width), F32))
    return pl.pallas_call(
        functools.partial(_lru_kernel, starts=starts, width=width),
        grid=(n_groups, batch, tpb),
        in_specs=[
            pl.BlockSpec((tt, group), lambda g, b, t: (row(g, b, t), g)),
            pl.BlockSpec((tt, group), lambda g, b, t: (row(g, b, t), n_groups + g)),
            pl.BlockSpec((bpg, width, 2 * width), lambda g, b, t: (g, 0, 0),
                         pipeline_mode=pl.Buffered(1)),
            pl.BlockSpec((SUBLANES, group), lambda g, b, t: (0, g)),
        ],
        out_specs=pl.BlockSpec((tt, group), lambda g, b, t: (row(g, b, t), g)),
        out_shape=jax.ShapeDtypeStruct((m, lru_width), MXU_DTYPE),
        scratch_shapes=scratch,
        compiler_params=_compiler_params(3, blocks, resident),
        name="rglru",
    )(proj, proj, w_win, params)


def _embed_gate_weights(w_r, w_i, group, width, starts):
    blk = w_r.shape[1]
    bpg = len(starts)
    out = []
    for n in range(w_r.shape[0]):
        off = (n % bpg) * blk - starts[n % bpg]
        pad = ((off, width - blk - off), (off, width - blk - off))
        out.append(jnp.concatenate([jnp.pad(w_r[n], pad), jnp.pad(w_i[n], pad)], axis=1))
    return jnp.stack(out).astype(MXU_DTYPE)


_NORM_BK = 4096
_F32_WEIGHT_TILES = dict(bm=2048, bn=512)


def _mlp(h, hg, ss, w_up, w_down, layer, next_gain):
    z = _matmul(hg, w_up, layer, MXU_DTYPE, epilogue="relu2", row_sumsq=ss)
    if next_gain is None:
        return _matmul(z, w_down, layer, F32, epilogue="residual", res=h), None, None
    return _matmul(z, w_down, layer, F32, epilogue="residual", res=h, next_gain=next_gain,
                   bk=_NORM_BK)


def _retention_layer(h, hg, ss, positions, w_in, w_out, layer, next_gain, batch, seq):
    d = h.shape[1]
    proj = _matmul(hg, w_in, layer, MXU_DTYPE, row_sumsq=ss, **_F32_WEIGHT_TILES)
    cos, sin = _rope_tables(positions, d // RET_HEADS // 2)
    o = _retention(proj, cos, sin, batch, seq, d)
    return _matmul(o, w_out, layer, F32, epilogue="residual", res=h, next_gain=next_gain,
                   bk=_NORM_BK)


def _lru_layer(h, hg, ss, w_in, conv_w, conv_b, w_rgate, b_rgate, w_igate, b_igate, lam, w_out,
               layer, next_gain, batch, seq):
    lru_width = w_out.shape[1]
    blk = lru_width // LRU_BLOCKS
    group = blk * LANES // math.gcd(blk, LANES)
    width, starts = _gate_windows(blk, group)
    w_win = _embed_gate_weights(w_rgate, w_igate, group, width, starts)
    params = jnp.concatenate(
        [conv_w] + [v.reshape(1, lru_width) for v in (conv_b, b_rgate, b_igate, lam)],
        axis=0).astype(F32)
    proj = _matmul(hg, w_in, layer, F32, row_sumsq=ss, **_F32_WEIGHT_TILES)
    mixed = _lru(proj, w_win, params, batch, seq, group, width, starts)
    return _matmul(mixed, w_out, layer, F32, epilogue="residual", res=h, next_gain=next_gain,
                   bk=_NORM_BK)


def kernel(x, positions, norm_mix_g, norm_mlp_g, final_norm_g, ret_w_in, ret_w_out,
           lru_w_in, lru_conv_w, lru_conv_b, lru_w_rgate, lru_b_rgate, lru_w_igate,
           lru_b_igate, lru_lambda, lru_w_out, mlp_w_up, mlp_w_down):
    batch, seq, d = x.shape
    h = x.reshape(batch * seq, d)
    depth = norm_mix_g.shape[0]
    n_mixers = 2
    cast = lambda w: w.astype(MXU_DTYPE)
    ret_w_out, lru_w_out, mlp_w_up, mlp_w_down = map(
        cast, (ret_w_out, lru_w_out, mlp_w_up, mlp_w_down))
    hg, ss = _norm_operand(h, norm_mix_g[0])
    for i in range(depth):
        j = i // n_mixers
        if i % n_mixers == 0:
            h, hg, ss = _retention_layer(h, hg, ss, positions, ret_w_in, ret_w_out, j,
                                         norm_mlp_g[i], batch, seq)
        else:
            h, hg, ss = _lru_layer(h, hg, ss, lru_w_in, lru_conv_w[j], lru_conv_b[j],
                                   lru_w_rgate[j], lru_b_rgate[j], lru_w_igate[j],
                                   lru_b_igate[j], lru_lambda[j], lru_w_out, j,
                                   norm_mlp_g[i], batch, seq)
        next_gain = norm_mix_g[i + 1] if i + 1 < depth else None
        h, hg, ss = _mlp(h, hg, ss, mlp_w_up, mlp_w_down, i, next_gain)
    return _rmsnorm(h, final_norm_g, x.dtype).reshape(batch, seq, d)
```

```python
import functools
import math

import jax
import jax.numpy as jnp
from jax import lax
from jax.experimental import pallas as pl
from jax.experimental.pallas import tpu as pltpu

RET_HEADS = 16
CHUNK = 64
ROPE_THETA = 10000.0
LRU_BLOCKS = 16
CONV_WIDTH = 4
LRU_C = 8.0
EPS = 1e-6

LANES = 128
SUBLANES = 8
MXU_WIDTH = 256
VMEM_BYTES_V7X = 64 * 1024 * 1024
VMEM_COMPILER_RESERVE = 2 * 1024 * 1024
VMEM_LIMIT_FLOOR = 16 * 1024 * 1024

MXU_DTYPE = jnp.bfloat16
F32 = jnp.float32

RET_SUB = 256
LRU_ROWS = 256
SQRT_FLOOR = 1e-30


def _round_up(x, m):
    return (x + m - 1) // m * m


def _pick_tile(dim, target, align):
    if dim <= target:
        return dim
    t = target // align * align
    while t >= align:
        if dim % t == 0:
            return t
        t -= align
    return dim


def _nbytes(shape, dtype):
    n = 1
    for s in shape:
        n *= s
    return n * jnp.dtype(dtype).itemsize


def _compiler_params(n_axes, pipelined_bytes, resident_bytes):
    limit = 2 * pipelined_bytes + resident_bytes
    limit = min(max(limit, VMEM_LIMIT_FLOOR), VMEM_BYTES_V7X - VMEM_COMPILER_RESERVE)
    return pltpu.CompilerParams(
        dimension_semantics=("arbitrary",) * n_axes, vmem_limit_bytes=int(limit))


def _rmsnorm_kernel(x_ref, g_ref, o_ref):
    x = x_ref[...]
    ms = jnp.mean(x * x, axis=-1, keepdims=True)
    o_ref[...] = ((x * lax.rsqrt(ms + EPS)) * g_ref[...]).astype(o_ref.dtype)


def _rmsnorm(x, g, out_dtype):
    m, d = x.shape
    rows = _pick_tile(m, 256, SUBLANES)
    blocks = _nbytes((rows, d), x.dtype) + _nbytes((rows, d), out_dtype)
    return pl.pallas_call(
        _rmsnorm_kernel,
        grid=(m // rows,),
        in_specs=[pl.BlockSpec((rows, d), lambda i: (i, 0)),
                  pl.BlockSpec((1, d), lambda i: (0, 0))],
        out_specs=pl.BlockSpec((rows, d), lambda i: (i, 0)),
        out_shape=jax.ShapeDtypeStruct((m, d), out_dtype),
        compiler_params=_compiler_params(1, blocks, 2 * _nbytes((rows, d), F32)),
        name="rmsnorm",
    )(x, g.reshape(1, d).astype(F32))


def _row_sumsq(x):
    return jnp.broadcast_to(jnp.sum(x * x, axis=-1, keepdims=True), (x.shape[0], LANES))


def _norm_operand_kernel(x_ref, g_ref, xg_ref, ss_ref):
    x = x_ref[...]
    xg_ref[...] = (x * g_ref[...]).astype(xg_ref.dtype)
    ss_ref[...] = _row_sumsq(x)


def _norm_operand(x, g):
    m, d = x.shape
    rows = _pick_tile(m, 256, SUBLANES)
    blocks = (_nbytes((rows, d), x.dtype) + _nbytes((rows, d), MXU_DTYPE)
              + _nbytes((rows, LANES), F32))
    return pl.pallas_call(
        _norm_operand_kernel,
        grid=(m // rows,),
        in_specs=[pl.BlockSpec((rows, d), lambda i: (i, 0)),
                  pl.BlockSpec((1, d), lambda i: (0, 0))],
        out_specs=[pl.BlockSpec((rows, d), lambda i: (i, 0)),
                   pl.BlockSpec((rows, LANES), lambda i: (i, 0))],
        out_shape=[jax.ShapeDtypeStruct((m, d), MXU_DTYPE),
                   jax.ShapeDtypeStruct((m, LANES), F32)],
        compiler_params=_compiler_params(1, blocks, 2 * _nbytes((rows, d), F32)),
        name="norm_operand",
    )(x, g.reshape(1, d).astype(F32))


def _matmul_kernel(*refs, nk, epilogue, scaled, emit_norm, inv_dim):
    refs = iter(refs)
    x_ref, w_ref = next(refs), next(refs)
    ss_ref = next(refs) if scaled else None
    r_ref = next(refs) if epilogue == "residual" else None
    g_ref = next(refs) if emit_norm else None
    o_ref = next(refs)
    og_ref, oss_ref = (next(refs), next(refs)) if emit_norm else (None, None)

    def product():
        acc = jnp.dot(x_ref[...], w_ref[...].astype(MXU_DTYPE), preferred_element_type=F32)
        if scaled:
            ss = ss_ref[...]
            total = ss[:, :LANES]
            for j in range(1, ss.shape[1] // LANES):
                total = total + ss[:, j * LANES:(j + 1) * LANES]
            scale = lax.rsqrt(total * inv_dim + EPS)
            acc = acc * jnp.tile(scale, (1, acc.shape[1] // LANES))
        return acc

    def store(out):
        o_ref[...] = out.astype(o_ref.dtype)
        if emit_norm:
            og_ref[...] = (out * g_ref[...]).astype(og_ref.dtype)
            oss_ref[...] = _row_sumsq(out)

    def first(last):
        acc = product()
        if epilogue == "relu2":
            z = jnp.maximum(acc, 0.0)
            acc = z * z
        elif epilogue == "residual":
            acc = r_ref[...] + acc
        if last:
            store(acc)
        else:
            o_ref[...] = acc

    if nk == 1:
        first(True)
        return

    k = pl.program_id(2)
    pl.when(k == 0)(functools.partial(first, False))

    if not emit_norm:
        @pl.when(k > 0)
        def _():
            o_ref[...] += product()
        return

    if nk > 2:
        @pl.when(jnp.logical_and(k > 0, k < nk - 1))
        def _():
            o_ref[...] += product()

    @pl.when(k == nk - 1)
    def _():
        store(o_ref[...] + product())


def _matmul(x, w, layer, out_dtype, epilogue="none", res=None, row_sumsq=None, next_gain=None,
            bm=1024, bn=1024, bk=4096, x_double_buffered=True):
    m, kdim = x.shape
    n = w.shape[2]
    bm = _pick_tile(m, bm, SUBLANES)
    bn = _pick_tile(n, bn, MXU_WIDTH if n % MXU_WIDTH == 0 else LANES)
    bk = _pick_tile(kdim, bk, LANES)
    nk = kdim // bk
    scaled = row_sumsq is not None
    emit_norm = next_gain is not None
    assert nk == 1 or (out_dtype == F32 and epilogue != "relu2" and not scaled)
    cast_w = w.dtype != MXU_DTYPE
    assert nk == 1 or not cast_w
    row_mode = dict(pipeline_mode=pl.Buffered(1)) if cast_w else {}
    x_mode = {} if x_double_buffered else row_mode
    in_specs = [pl.BlockSpec((bm, bk), lambda i, j, k: (i, k), **x_mode),
                pl.BlockSpec((None, bk, bn), lambda i, j, k: (layer, k, j))]
    args = [x, w]
    blocks = _nbytes((bk, bn), w.dtype) + _nbytes((bm, bn), out_dtype)
    temps = 3 * _nbytes((bm, bn), F32)

    def add_window(nbytes, mode):
        nonlocal blocks, temps
        if mode:
            temps += nbytes
        else:
            blocks += nbytes

    add_window(_nbytes((bm, bk), x.dtype), x_mode)
    if scaled:
        in_specs.append(pl.BlockSpec((bm, row_sumsq.shape[1]), lambda i, j, k: (i, 0), **row_mode))
        args.append(row_sumsq)
        add_window(_nbytes((bm, row_sumsq.shape[1]), F32), row_mode)
    if cast_w:
        temps += _nbytes((bk, bn), MXU_DTYPE)
    if epilogue == "residual":
        in_specs.append(pl.BlockSpec((bm, bn), lambda i, j, k: (i, j)))
        args.append(res)
        blocks += _nbytes((bm, bn), res.dtype)
    out_specs = pl.BlockSpec((bm, bn), lambda i, j, k: (i, j))
    out_shape = jax.ShapeDtypeStruct((m, n), out_dtype)
    if emit_norm:
        in_specs.append(pl.BlockSpec((1, bn), lambda i, j, k: (0, j)))
        args.append(next_gain.reshape(1, n).astype(F32))
        out_specs = [out_specs, pl.BlockSpec((bm, bn), lambda i, j, k: (i, j)),
                     pl.BlockSpec((bm, LANES), lambda i, j, k: (i, j))]
        out_shape = [out_shape, jax.ShapeDtypeStruct((m, n), MXU_DTYPE),
                     jax.ShapeDtypeStruct((m, n // bn * LANES), F32)]
        blocks += _nbytes((bm, bn), MXU_DTYPE) + _nbytes((bm, LANES), F32)
    return pl.pallas_call(
        functools.partial(_matmul_kernel, nk=nk, epilogue=epilogue, scaled=scaled,
                          emit_norm=emit_norm, inv_dim=1.0 / kdim),
        grid=(m // bm, n // bn, nk),
        in_specs=in_specs,
        out_specs=out_specs,
        out_shape=out_shape,
        compiler_params=_compiler_params(3, blocks, temps),
        name="matmul_" + epilogue,
    )(*args)


def _rope_table_kernel(pos_ref, freq_ref, cos_ref, sin_ref):
    ang = pos_ref[...].astype(F32) * freq_ref[...]
    cos_ref[...] = jnp.cos(ang)
    sin_ref[...] = jnp.sin(ang)


def _rope_tables(positions, half):
    m = positions.size
    rows = _pick_tile(m, 1024, SUBLANES)
    inv_freq = ROPE_THETA ** (-jnp.arange(half, dtype=F32) / half)
    out = jax.ShapeDtypeStruct((m, half), F32)
    return pl.pallas_call(
        _rope_table_kernel,
        grid=(m // rows,),
        in_specs=[pl.BlockSpec((rows, 1), lambda i: (i, 0)),
                  pl.BlockSpec((1, half), lambda i: (0, 0))],
        out_specs=[pl.BlockSpec((rows, half), lambda i: (i, 0))] * 2,
        out_shape=[out, out],
        compiler_params=_compiler_params(1, 3 * _nbytes((rows, LANES), F32),
                                         4 * _nbytes((rows, LANES), F32)),
        name="rope_tables",
    )(positions.reshape(m, 1), inv_freq.reshape(1, half))


def _retention_kernel(dec_ref, q_ref, k_ref, v_ref, g_ref, cos_ref, sin_ref, mask_ref,
                      qd_ref, kd_ref, o_ref, state_ref, *, n_sub, scale):
    head = pl.program_id(1)

    @pl.when(pl.program_id(2) == 0)
    def _():
        state_ref[...] = jnp.zeros_like(state_ref)

    dec = dec_ref[head]
    mask = mask_ref[0]
    qd = qd_ref[0]
    kd = kd_ref[0]
    sub = mask.shape[0]
    half = q_ref.shape[-1] // 2

    def rope(t, cos, sin):
        t1, t2 = t[:, :half], t[:, half:]
        return jnp.concatenate([t1 * cos - t2 * sin, t2 * cos + t1 * sin], axis=-1)

    for s in range(n_sub):
        rows = pl.ds(s * sub, sub)
        cos = cos_ref[rows, :]
        sin = sin_ref[rows, :]
        q = rope(q_ref[rows, :].astype(F32), cos, sin)
        k = rope(k_ref[rows, :].astype(F32), cos, sin) * scale
        v = v_ref[rows, :]
        qb = q.astype(MXU_DTYPE)
        kb = k.astype(MXU_DTYPE)
        scores = lax.dot_general(qb, kb, (((1,), (1,)), ((), ())),
                                 preferred_element_type=F32) * mask
        state = state_ref[...]
        o = (jnp.dot(scores.astype(MXU_DTYPE), v, preferred_element_type=F32)
             + jnp.dot(qb, state.astype(MXU_DTYPE), preferred_element_type=F32) * qd)
        kdb = (k * kd).astype(MXU_DTYPE)
        state_ref[...] = state * dec + lax.dot_general(
            kdb, v, (((0,), (0,)), ((), ())), preferred_element_type=F32)
        ms = jnp.mean(o * o, axis=-1, keepdims=True)
        of = o * lax.rsqrt(ms + EPS)
        g = g_ref[rows, :].astype(F32)
        o_ref[rows, :] = (of * (g * jax.nn.sigmoid(g))).astype(o_ref.dtype)


def _retention(proj, cos, sin, batch, seq, d_model):
    m = proj.shape[0]
    heads = RET_HEADS
    dk = d_model // heads
    dv = (proj.shape[1] - 2 * d_model) // (2 * heads)
    sub = _pick_tile(seq, RET_SUB, CHUNK)
    tt = _pick_tile(seq, 1024, sub)
    n_sub = tt // sub
    tpb = seq // tt
    v_blk0 = 2 * d_model // dv

    log_g = jnp.log1p(-jnp.exp2(-5.0 - jnp.arange(heads, dtype=F32)))
    idx = jnp.arange(sub, dtype=F32)
    dist = jnp.abs(idx[:, None] - idx[None, :])
    chunk_id = jnp.arange(sub) // CHUNK
    visible = chunk_id[None, :] <= chunk_id[:, None]
    mask = jnp.where(visible[None], jnp.exp(log_g[:, None, None] * dist[None]), 0.0)
    q_dec = jnp.exp(log_g[:, None] * (idx + 1.0))[:, :, None]
    k_dec = jnp.exp(log_g[:, None] * (sub - 1.0 - idx))[:, :, None]
    blk_dec = jnp.exp(log_g * sub)

    row = lambda b, h, t: b * tpb + t
    blocks = (2 * _nbytes((tt, dk), proj.dtype) + 2 * _nbytes((tt, dv), proj.dtype)
              + 2 * _nbytes((tt, dk // 2), F32)
              + _nbytes((sub, sub), F32) + 2 * _nbytes((sub, LANES), F32)
              + _nbytes((tt, dv), MXU_DTYPE))
    temps = _nbytes((dk, dv), F32) + 8 * _nbytes((sub, dv), F32)
    return pl.pallas_call(
        functools.partial(_retention_kernel, n_sub=n_sub, scale=dk ** -0.5),
        grid=(batch, heads, tpb),
        in_specs=[
            pl.BlockSpec(memory_space=pltpu.SMEM),
            pl.BlockSpec((tt, dk), lambda b, h, t: (row(b, h, t), h)),
            pl.BlockSpec((tt, dk), lambda b, h, t: (row(b, h, t), heads + h)),
            pl.BlockSpec((tt, dv), lambda b, h, t: (row(b, h, t), v_blk0 + h)),
            pl.BlockSpec((tt, dv), lambda b, h, t: (row(b, h, t), v_blk0 + heads + h)),
            pl.BlockSpec((tt, dk // 2), lambda b, h, t: (row(b, h, t), 0)),
            pl.BlockSpec((tt, dk // 2), lambda b, h, t: (row(b, h, t), 0)),
            pl.BlockSpec((1, sub, sub), lambda b, h, t: (h, 0, 0)),
            pl.BlockSpec((1, sub, 1), lambda b, h, t: (h, 0, 0)),
            pl.BlockSpec((1, sub, 1), lambda b, h, t: (h, 0, 0)),
        ],
        out_specs=pl.BlockSpec((tt, dv), lambda b, h, t: (row(b, h, t), h)),
        out_shape=jax.ShapeDtypeStruct((m, heads * dv), MXU_DTYPE),
        scratch_shapes=[pltpu.VMEM((dk, dv), F32)],
        compiler_params=_compiler_params(3, blocks, temps),
        name="retention",
    )(blk_dec, proj, proj, proj, proj, cos, sin, mask, q_dec, k_dec)


_P_CONV_B, _P_BIAS_R, _P_BIAS_I, _P_LAMBDA = CONV_WIDTH, CONV_WIDTH + 1, CONV_WIDTH + 2, CONV_WIDTH + 3


def _gate_windows(blk, group):
    width = min(group, _round_up(blk, LANES) + LANES)
    starts = tuple(min(n * blk // LANES * LANES, group - width) for n in range(group // blk))
    return width, starts


def _gelu_tanh(x):
    c = 0.7978845608028654
    return x * (0.5 * (1.0 + jnp.tanh(c * (x + 0.044715 * (x * x * x)))))


def _lru_kernel(x_ref, gb_ref, w_ref, p_ref, o_ref, xs_ref, gs_ref, xc_ref, xb_ref, gates_ref,
                os_ref, h_ref, *, starts, width):
    tt, gw = x_ref.shape
    hist = SUBLANES
    n_tiles = tt // SUBLANES
    n_slabs = gw // LANES

    def phase(ref, slab, first_row):
        return ref[slab, pl.ds(first_row, n_tiles, stride=SUBLANES), :]

    @pl.when(pl.program_id(2) == 0)
    def _():
        h_ref[...] = jnp.zeros_like(h_ref)
        xs_ref[:, 0:hist, :] = jnp.zeros((n_slabs, hist, LANES), F32)

    for s in range(n_slabs):
        lanes = slice(s * LANES, (s + 1) * LANES)
        xs_ref[s, hist:hist + tt, :] = x_ref[:, lanes]
        gs_ref[s] = gb_ref[:, lanes]
    p = p_ref[...]

    for s in range(n_slabs):
        lanes = slice(s * LANES, (s + 1) * LANES)
        taps = {o: phase(xs_ref, s, o) for o in range(hist - (CONV_WIDTH - 1), hist + SUBLANES)}
        for j in range(SUBLANES):
            xc = p[_P_CONV_B:_P_CONV_B + 1, lanes]
            for k in range(CONV_WIDTH):
                xc = xc + p[k:k + 1, lanes] * taps[hist + j - (CONV_WIDTH - 1) + k]
            xc_ref[s, j] = xc
            xb_ref[j * n_tiles:(j + 1) * n_tiles, lanes] = xc.astype(MXU_DTYPE)

    written = set()
    for n, ws in enumerate(starts):
        g = jnp.dot(xb_ref[:, ws:ws + width], w_ref[n], preferred_element_type=F32)
        for c in range(0, width, LANES):
            for src, dst in ((c, ws + c), (width + c, gw + ws + c)):
                if dst in written:
                    gates_ref[:, dst:dst + LANES] += g[:, src:src + LANES]
                else:
                    gates_ref[:, dst:dst + LANES] = g[:, src:src + LANES]
                    written.add(dst)
    assert len(written) == 2 * n_slabs

    tile = lax.broadcasted_iota(jnp.int32, (n_tiles, LANES), 0)
    for s in range(n_slabs):
        lanes = slice(s * LANES, (s + 1) * LANES)
        lanes_i = slice(gw + s * LANES, gw + (s + 1) * LANES)
        bias_r = p[_P_BIAS_R:_P_BIAS_R + 1, lanes]
        bias_i = p[_P_BIAS_I:_P_BIAS_I + 1, lanes]
        neg_lam = -p[_P_LAMBDA:_P_LAMBDA + 1, lanes]
        softplus = jnp.maximum(neg_lam, 0.0) + jnp.log1p(jnp.exp(-jnp.abs(neg_lam)))

        h_loc, a_cum = [], []
        for j in range(SUBLANES):
            rows = slice(j * n_tiles, (j + 1) * n_tiles)
            xc = xc_ref[s, j]
            r = jax.nn.sigmoid(gates_ref[rows, lanes] + bias_r)
            i = jax.nn.sigmoid(gates_ref[rows, lanes_i] + bias_i)
            log_a = (-LRU_C * r) * softplus
            a = jnp.exp(log_a)
            y = 1.0 - a * a
            u = (y * lax.rsqrt(jnp.maximum(y, SQRT_FLOOR))) * (i * xc)
            h_loc.append(u if j == 0 else a * h_loc[-1] + u)
            a_cum.append(a if j == 0 else a * a_cum[-1])

        a_inc, u_inc = a_cum[-1], h_loc[-1]
        d = 1
        while d < n_tiles:
            keep = tile >= d
            a_prev = jnp.where(keep, pltpu.roll(a_inc, d, 0), 1.0)
            u_prev = jnp.where(keep, pltpu.roll(u_inc, d, 0), 0.0)
            u_inc = u_inc + a_inc * u_prev
            a_inc = a_inc * a_prev
            d *= 2
        h_prev = jnp.broadcast_to(h_ref[s], (n_tiles, LANES))
        h_end = u_inc + a_inc * h_prev
        h_in = jnp.where(tile >= 1, pltpu.roll(h_end, 1, 0), h_prev)
        h_ref[s] = h_end[n_tiles - 1:n_tiles, :]

        for j in range(SUBLANES):
            hs = h_loc[j] + a_cum[j] * h_in
            os_ref[s, pl.ds(j, n_tiles, stride=SUBLANES), :] = hs * _gelu_tanh(phase(gs_ref, s, j))
        o_ref[:, lanes] = os_ref[s].astype(o_ref.dtype)

    xs_ref[:, 0:hist, :] = xs_ref[:, tt:tt + hist, :]


def _lru(proj, w_win, params, batch, seq, group, width, starts):
    m = proj.shape[0]
    lru_width = proj.shape[1] // 2
    n_groups = lru_width // group
    n_slabs = group // LANES
    bpg = len(starts)
    tt = _pick_tile(seq, LRU_ROWS, SUBLANES * SUBLANES)
    assert tt % (SUBLANES * SUBLANES) == 0
    tpb = seq // tt
    row = lambda g, b, t: b * tpb + t
    blocks = (2 * _nbytes((tt, group), proj.dtype) + _nbytes((SUBLANES, group), F32)
              + _nbytes((tt, group), MXU_DTYPE))
    scratch = [pltpu.VMEM((n_slabs, tt + SUBLANES, LANES), F32),
               pltpu.VMEM((n_slabs, tt, LANES), F32),
               pltpu.VMEM((n_slabs, SUBLANES, tt // SUBLANES, LANES), F32),
               pltpu.VMEM((tt, group), MXU_DTYPE),
               pltpu.VMEM((tt, 2 * group), F32),
               pltpu.VMEM((n_slabs, tt, LANES), F32),
               pltpu.VMEM((n_slabs, 1, LANES), F32)]
    resident = (_nbytes((bpg, width, 2 * width), MXU_DTYPE) + 4 * _nbytes((tt + SUBLANES, group), F32)
                + _nbytes((tt, group), MXU_DTYPE) + _nbytes((tt, 2 * group), F32)
                + 2 * _nbytes((tt, 2 * width), F32))
    return pl.pallas_call(
        functools.partial(_lru_kernel, starts=starts, width=width),
        grid=(n_groups, batch, tpb),
        in_specs=[
            pl.BlockSpec((tt, group), lambda g, b, t: (row(g, b, t), g)),
            pl.BlockSpec((tt, group), lambda g, b, t: (row(g, b, t), n_groups + g)),
            pl.BlockSpec((bpg, width, 2 * width), lambda g, b, t: (g, 0, 0),
                         pipeline_mode=pl.Buffered(1)),
            pl.BlockSpec((SUBLANES, group), lambda g, b, t: (0, g)),
        ],
        out_specs=pl.BlockSpec((tt, group), lambda g, b, t: (row(g, b, t), g)),
        out_shape=jax.ShapeDtypeStruct((m, lru_width), MXU_DTYPE),
        scratch_shapes=scratch,
        compiler_params=_compiler_params(3, blocks, resident),
        name="rglru",
    )(proj, proj, w_win, params)


def _embed_gate_weights(w_r, w_i, group, width, starts):
    blk = w_r.shape[1]
    bpg = len(starts)
    out = []
    for n in range(w_r.shape[0]):
        off = (n % bpg) * blk - starts[n % bpg]
        pad = ((off, width - blk - off), (off, width - blk - off))
        out.append(jnp.concatenate([jnp.pad(w_r[n], pad), jnp.pad(w_i[n], pad)], axis=1))
    return jnp.stack(out).astype(MXU_DTYPE)


_NORM_BK = 4096
_F32_WEIGHT_TILES = dict(bm=2048, bn=512)


def _mlp(h, hg, ss, w_up, w_down, layer, next_gain):
    z = _matmul(hg, w_up, layer, MXU_DTYPE, epilogue="relu2", row_sumsq=ss)
    if next_gain is None:
        return _matmul(z, w_down, layer, F32, epilogue="residual", res=h), None, None
    return _matmul(z, w_down, layer, F32, epilogue="residual", res=h, next_gain=next_gain,
                   bk=_NORM_BK)


def _retention_layer(h, hg, ss, positions, w_in, w_out, layer, next_gain, batch, seq):
    d = h.shape[1]
    proj = _matmul(hg, w_in, layer, MXU_DTYPE, row_sumsq=ss, **_F32_WEIGHT_TILES)
    cos, sin = _rope_tables(positions, d // RET_HEADS // 2)
    o = _retention(proj, cos, sin, batch, seq, d)
    return _matmul(o, w_out, layer, F32, epilogue="residual", res=h, next_gain=next_gain,
                   bk=_NORM_BK)


def _lru_layer(h, hg, ss, w_in, conv_w, conv_b, w_rgate, b_rgate, w_igate, b_igate, lam, w_out,
               layer, next_gain, batch, seq):
    lru_width = w_out.shape[1]
    blk = lru_width // LRU_BLOCKS
    group = blk * LANES // math.gcd(blk, LANES)
    width, starts = _gate_windows(blk, group)
    w_win = _embed_gate_weights(w_rgate, w_igate, group, width, starts)
    params = jnp.concatenate(
        [conv_w] + [v.reshape(1, lru_width) for v in (conv_b, b_rgate, b_igate, lam)],
        axis=0).astype(F32)
    proj = _matmul(hg, w_in, layer, F32, row_sumsq=ss, x_double_buffered=False, **_F32_WEIGHT_TILES)
    mixed = _lru(proj, w_win, params, batch, seq, group, width, starts)
    return _matmul(mixed, w_out, layer, F32, epilogue="residual", res=h, next_gain=next_gain,
                   bk=_NORM_BK)


def kernel(x, positions, norm_mix_g, norm_mlp_g, final_norm_g, ret_w_in, ret_w_out,
           lru_w_in, lru_conv_w, lru_conv_b, lru_w_rgate, lru_b_rgate, lru_w_igate,
           lru_b_igate, lru_lambda, lru_w_out, mlp_w_up, mlp_w_down):
    batch, seq, d = x.shape
    h = x.reshape(batch * seq, d)
    depth = norm_mix_g.shape[0]
    n_mixers = 2
    cast = lambda w: w.astype(MXU_DTYPE)
    ret_w_out, lru_w_out, mlp_w_up, mlp_w_down = map(
        cast, (ret_w_out, lru_w_out, mlp_w_up, mlp_w_down))
    hg, ss = _norm_operand(h, norm_mix_g[0])
    for i in range(depth):
        j = i // n_mixers
        if i % n_mixers == 0:
            h, hg, ss = _retention_layer(h, hg, ss, positions, ret_w_in, ret_w_out, j,
                                         norm_mlp_g[i], batch, seq)
        else:
            h, hg, ss = _lru_layer(h, hg, ss, lru_w_in, lru_conv_w[j], lru_conv_b[j],
                                   lru_w_rgate[j], lru_b_rgate[j], lru_w_igate[j],
                                   lru_b_igate[j], lru_lambda[j], lru_w_out, j,
                                   norm_mlp_g[i], batch, seq)
        next_gain = norm_mix_g[i + 1] if i + 1 < depth else None
        h, hg, ss = _mlp(h, hg, ss, mlp_w_up, mlp_w_down, i, next_gain)
    return _rmsnorm(h, final_norm_g, x.dtype).reshape(batch, seq, d)
```

```python
import functools
import math

import jax
import jax.numpy as jnp
from jax import lax
from jax.experimental import pallas as pl
from jax.experimental.pallas import tpu as pltpu

RET_HEADS = 16
CHUNK = 64
ROPE_THETA = 10000.0
LRU_BLOCKS = 16
CONV_WIDTH = 4
LRU_C = 8.0
EPS = 1e-6

LANES = 128
SUBLANES = 8
MXU_WIDTH = 256
VMEM_BYTES_V7X = 64 * 1024 * 1024
VMEM_COMPILER_RESERVE = 2 * 1024 * 1024
VMEM_LIMIT_FLOOR = 16 * 1024 * 1024

MXU_DTYPE = jnp.bfloat16
F32 = jnp.float32

RET_SUB = 256
LRU_ROWS = 256
SQRT_FLOOR = 1e-30


def _round_up(x, m):
    return (x + m - 1) // m * m


def _pick_tile(dim, target, align):
    if dim <= target:
        return dim
    t = target // align * align
    while t >= align:
        if dim % t == 0:
            return t
        t -= align
    return dim


def _nbytes(shape, dtype):
    n = 1
    for s in shape:
        n *= s
    return n * jnp.dtype(dtype).itemsize


def _compiler_params(n_axes, pipelined_bytes, resident_bytes):
    limit = 2 * pipelined_bytes + resident_bytes
    limit = min(max(limit, VMEM_LIMIT_FLOOR), VMEM_BYTES_V7X - VMEM_COMPILER_RESERVE)
    return pltpu.CompilerParams(
        dimension_semantics=("arbitrary",) * n_axes, vmem_limit_bytes=int(limit))


def _rmsnorm_kernel(x_ref, g_ref, o_ref):
    x = x_ref[...]
    ms = jnp.mean(x * x, axis=-1, keepdims=True)
    o_ref[...] = ((x * lax.rsqrt(ms + EPS)) * g_ref[...]).astype(o_ref.dtype)


def _rmsnorm(x, g, out_dtype):
    m, d = x.shape
    rows = _pick_tile(m, 256, SUBLANES)
    blocks = _nbytes((rows, d), x.dtype) + _nbytes((rows, d), out_dtype)
    return pl.pallas_call(
        _rmsnorm_kernel,
        grid=(m // rows,),
        in_specs=[pl.BlockSpec((rows, d), lambda i: (i, 0)),
                  pl.BlockSpec((1, d), lambda i: (0, 0))],
        out_specs=pl.BlockSpec((rows, d), lambda i: (i, 0)),
        out_shape=jax.ShapeDtypeStruct((m, d), out_dtype),
        compiler_params=_compiler_params(1, blocks, 2 * _nbytes((rows, d), F32)),
        name="rmsnorm",
    )(x, g.reshape(1, d).astype(F32))


def _row_sumsq(x):
    return jnp.broadcast_to(jnp.sum(x * x, axis=-1, keepdims=True), (x.shape[0], LANES))


def _norm_operand_kernel(x_ref, g_ref, xg_ref, ss_ref):
    x = x_ref[...]
    xg_ref[...] = (x * g_ref[...]).astype(xg_ref.dtype)
    ss_ref[...] = _row_sumsq(x)


def _norm_operand(x, g):
    m, d = x.shape
    rows = _pick_tile(m, 256, SUBLANES)
    blocks = (_nbytes((rows, d), x.dtype) + _nbytes((rows, d), MXU_DTYPE)
              + _nbytes((rows, LANES), F32))
    return pl.pallas_call(
        _norm_operand_kernel,
        grid=(m // rows,),
        in_specs=[pl.BlockSpec((rows, d), lambda i: (i, 0)),
                  pl.BlockSpec((1, d), lambda i: (0, 0))],
        out_specs=[pl.BlockSpec((rows, d), lambda i: (i, 0)),
                   pl.BlockSpec((rows, LANES), lambda i: (i, 0))],
        out_shape=[jax.ShapeDtypeStruct((m, d), MXU_DTYPE),
                   jax.ShapeDtypeStruct((m, LANES), F32)],
        compiler_params=_compiler_params(1, blocks, 2 * _nbytes((rows, d), F32)),
        name="norm_operand",
    )(x, g.reshape(1, d).astype(F32))


def _matmul_kernel(*refs, nk, epilogue, scaled, emit_norm, inv_dim):
    refs = iter(refs)
    x_ref, w_ref = next(refs), next(refs)
    ss_ref = next(refs) if scaled else None
    r_ref = next(refs) if epilogue == "residual" else None
    g_ref = next(refs) if emit_norm else None
    o_ref = next(refs)
    og_ref, oss_ref = (next(refs), next(refs)) if emit_norm else (None, None)

    def product():
        acc = jnp.dot(x_ref[...], w_ref[...].astype(MXU_DTYPE), preferred_element_type=F32)
        if scaled:
            ss = ss_ref[...]
            total = ss[:, :LANES]
            for j in range(1, ss.shape[1] // LANES):
                total = total + ss[:, j * LANES:(j + 1) * LANES]
            scale = lax.rsqrt(total * inv_dim + EPS)
            acc = acc * jnp.tile(scale, (1, acc.shape[1] // LANES))
        return acc

    def store(out):
        o_ref[...] = out.astype(o_ref.dtype)
        if emit_norm:
            og_ref[...] = (out * g_ref[...]).astype(og_ref.dtype)
            oss_ref[...] = _row_sumsq(out)

    def first(last):
        acc = product()
        if epilogue == "relu2":
            z = jnp.maximum(acc, 0.0)
            acc = z * z
        elif epilogue == "residual":
            acc = r_ref[...] + acc
        if last:
            store(acc)
        else:
            o_ref[...] = acc

    if nk == 1:
        first(True)
        return

    k = pl.program_id(2)
    pl.when(k == 0)(functools.partial(first, False))

    if not emit_norm:
        @pl.when(k > 0)
        def _():
            o_ref[...] += product()
        return

    if nk > 2:
        @pl.when(jnp.logical_and(k > 0, k < nk - 1))
        def _():
            o_ref[...] += product()

    @pl.when(k == nk - 1)
    def _():
        store(o_ref[...] + product())


def _matmul(x, w, layer, out_dtype, epilogue="none", res=None, row_sumsq=None, next_gain=None,
            bm=1024, bn=1024, bk=4096, x_double_buffered=True):
    m, kdim = x.shape
    n = w.shape[2]
    bm = _pick_tile(m, bm, SUBLANES)
    bn = _pick_tile(n, bn, MXU_WIDTH if n % MXU_WIDTH == 0 else LANES)
    bk = _pick_tile(kdim, bk, LANES)
    nk = kdim // bk
    scaled = row_sumsq is not None
    emit_norm = next_gain is not None
    assert nk == 1 or (out_dtype == F32 and epilogue != "relu2" and not scaled)
    cast_w = w.dtype != MXU_DTYPE
    assert nk == 1 or not cast_w
    row_mode = dict(pipeline_mode=pl.Buffered(1)) if cast_w else {}
    x_mode = {} if x_double_buffered else row_mode
    in_specs = [pl.BlockSpec((bm, bk), lambda i, j, k: (i, k), **x_mode),
                pl.BlockSpec((None, bk, bn), lambda i, j, k: (layer, k, j))]
    args = [x, w]
    blocks = _nbytes((bk, bn), w.dtype) + _nbytes((bm, bn), out_dtype)
    temps = 3 * _nbytes((bm, bn), F32)

    def add_window(nbytes, mode):
        nonlocal blocks, temps
        if mode:
            temps += nbytes
        else:
            blocks += nbytes

    add_window(_nbytes((bm, bk), x.dtype), x_mode)
    if scaled:
        in_specs.append(pl.BlockSpec((bm, row_sumsq.shape[1]), lambda i, j, k: (i, 0), **row_mode))
        args.append(row_sumsq)
        add_window(_nbytes((bm, row_sumsq.shape[1]), F32), row_mode)
    if cast_w:
        temps += _nbytes((bk, bn), MXU_DTYPE)
    if epilogue == "residual":
        in_specs.append(pl.BlockSpec((bm, bn), lambda i, j, k: (i, j)))
        args.append(res)
        blocks += _nbytes((bm, bn), res.dtype)
    out_specs = pl.BlockSpec((bm, bn), lambda i, j, k: (i, j))
    out_shape = jax.ShapeDtypeStruct((m, n), out_dtype)
    if emit_norm:
        in_specs.append(pl.BlockSpec((1, bn), lambda i, j, k: (0, j)))
        args.append(next_gain.reshape(1, n).astype(F32))
        out_specs = [out_specs, pl.BlockSpec((bm, bn), lambda i, j, k: (i, j)),
                     pl.BlockSpec((bm, LANES), lambda i, j, k: (i, j))]
        out_shape = [out_shape, jax.ShapeDtypeStruct((m, n), MXU_DTYPE),
                     jax.ShapeDtypeStruct((m, n // bn * LANES), F32)]
        blocks += _nbytes((bm, bn), MXU_DTYPE) + _nbytes((bm, LANES), F32)
    return pl.pallas_call(
        functools.partial(_matmul_kernel, nk=nk, epilogue=epilogue, scaled=scaled,
                          emit_norm=emit_norm, inv_dim=1.0 / kdim),
        grid=(m // bm, n // bn, nk),
        in_specs=in_specs,
        out_specs=out_specs,
        out_shape=out_shape,
        compiler_params=_compiler_params(3, blocks, temps),
        name="matmul_" + epilogue,
    )(*args)


def _rope_table_kernel(pos_ref, freq_ref, cos_ref, sin_ref):
    ang = pos_ref[...].astype(F32) * freq_ref[...]
    cos_ref[...] = jnp.cos(ang)
    sin_ref[...] = jnp.sin(ang)


def _rope_tables(positions, half):
    m = positions.size
    rows = _pick_tile(m, 1024, SUBLANES)
    inv_freq = ROPE_THETA ** (-jnp.arange(half, dtype=F32) / half)
    out = jax.ShapeDtypeStruct((m, half), F32)
    return pl.pallas_call(
        _rope_table_kernel,
        grid=(m // rows,),
        in_specs=[pl.BlockSpec((rows, 1), lambda i: (i, 0)),
                  pl.BlockSpec((1, half), lambda i: (0, 0))],
        out_specs=[pl.BlockSpec((rows, half), lambda i: (i, 0))] * 2,
        out_shape=[out, out],
        compiler_params=_compiler_params(1, 3 * _nbytes((rows, LANES), F32),
                                         4 * _nbytes((rows, LANES), F32)),
        name="rope_tables",
    )(positions.reshape(m, 1), inv_freq.reshape(1, half))


def _retention_kernel(dec_ref, q_ref, k_ref, v_ref, g_ref, cos_ref, sin_ref, mask_ref,
                      qd_ref, kd_ref, o_ref, state_ref, *, n_sub, scale):
    head = pl.program_id(2)

    @pl.when(pl.program_id(1) == 0)
    def _():
        state_ref[head] = jnp.zeros(state_ref.shape[1:], F32)

    dec = dec_ref[head]
    mask = mask_ref[head]
    qd = qd_ref[head]
    kd = kd_ref[head]
    sub = mask.shape[0]
    half = q_ref.shape[-1] // 2

    def rope(t, cos, sin):
        t1, t2 = t[:, :half], t[:, half:]
        return jnp.concatenate([t1 * cos - t2 * sin, t2 * cos + t1 * sin], axis=-1)

    for s in range(n_sub):
        rows = pl.ds(s * sub, sub)
        cos = cos_ref[rows, :]
        sin = sin_ref[rows, :]
        q = rope(q_ref[rows, :].astype(F32), cos, sin)
        k = rope(k_ref[rows, :].astype(F32), cos, sin) * scale
        v = v_ref[rows, :]
        qb = q.astype(MXU_DTYPE)
        kb = k.astype(MXU_DTYPE)
        scores = lax.dot_general(qb, kb, (((1,), (1,)), ((), ())),
                                 preferred_element_type=F32) * mask
        state = state_ref[head]
        o = (jnp.dot(scores.astype(MXU_DTYPE), v, preferred_element_type=F32)
             + jnp.dot(qb, state.astype(MXU_DTYPE), preferred_element_type=F32) * qd)
        kdb = (k * kd).astype(MXU_DTYPE)
        state_ref[head] = state * dec + lax.dot_general(
            kdb, v, (((0,), (0,)), ((), ())), preferred_element_type=F32)
        ms = jnp.mean(o * o, axis=-1, keepdims=True)
        of = o * lax.rsqrt(ms + EPS)
        g = g_ref[rows, :].astype(F32)
        o_ref[rows, :] = (of * (g * jax.nn.sigmoid(g))).astype(o_ref.dtype)


def _retention(proj, cos, sin, batch, seq, d_model):
    m = proj.shape[0]
    heads = RET_HEADS
    dk = d_model // heads
    dv = (proj.shape[1] - 2 * d_model) // (2 * heads)
    sub = _pick_tile(seq, RET_SUB, CHUNK)
    tt = _pick_tile(seq, 1024, sub)
    n_sub = tt // sub
    tpb = seq // tt
    v_blk0 = 2 * d_model // dv

    log_g = jnp.log1p(-jnp.exp2(-5.0 - jnp.arange(heads, dtype=F32)))
    idx = jnp.arange(sub, dtype=F32)
    dist = jnp.abs(idx[:, None] - idx[None, :])
    chunk_id = jnp.arange(sub) // CHUNK
    visible = chunk_id[None, :] <= chunk_id[:, None]
    mask = jnp.where(visible[None], jnp.exp(log_g[:, None, None] * dist[None]), 0.0)
    q_dec = jnp.exp(log_g[:, None] * (idx + 1.0))[:, :, None]
    k_dec = jnp.exp(log_g[:, None] * (sub - 1.0 - idx))[:, :, None]
    blk_dec = jnp.exp(log_g * sub)

    row = lambda b, t, h: b * tpb + t
    blocks = (2 * _nbytes((tt, dk), proj.dtype) + 2 * _nbytes((tt, dv), proj.dtype)
              + 2 * _nbytes((tt, dk // 2), F32)
              + heads * (_nbytes((sub, sub), F32) + 2 * _nbytes((sub, LANES), F32))
              + _nbytes((tt, dv), MXU_DTYPE))
    temps = heads * _nbytes((dk, dv), F32) + 8 * _nbytes((sub, dv), F32)
    return pl.pallas_call(
        functools.partial(_retention_kernel, n_sub=n_sub, scale=dk ** -0.5),
        grid=(batch, tpb, heads),
        in_specs=[
            pl.BlockSpec(memory_space=pltpu.SMEM),
            pl.BlockSpec((tt, dk), lambda b, t, h: (row(b, t, h), h)),
            pl.BlockSpec((tt, dk), lambda b, t, h: (row(b, t, h), heads + h)),
            pl.BlockSpec((tt, dv), lambda b, t, h: (row(b, t, h), v_blk0 + h)),
            pl.BlockSpec((tt, dv), lambda b, t, h: (row(b, t, h), v_blk0 + heads + h)),
            pl.BlockSpec((tt, dk // 2), lambda b, t, h: (row(b, t, h), 0)),
            pl.BlockSpec((tt, dk // 2), lambda b, t, h: (row(b, t, h), 0)),
            pl.BlockSpec((heads, sub, sub), lambda b, t, h: (0, 0, 0)),
            pl.BlockSpec((heads, sub, 1), lambda b, t, h: (0, 0, 0)),
            pl.BlockSpec((heads, sub, 1), lambda b, t, h: (0, 0, 0)),
        ],
        out_specs=pl.BlockSpec((tt, dv), lambda b, t, h: (row(b, t, h), h)),
        out_shape=jax.ShapeDtypeStruct((m, heads * dv), MXU_DTYPE),
        scratch_shapes=[pltpu.VMEM((heads, dk, dv), F32)],
        compiler_params=_compiler_params(3, blocks, temps),
        name="retention",
    )(blk_dec, proj, proj, proj, proj, cos, sin, mask, q_dec, k_dec)


_P_CONV_B, _P_BIAS_R, _P_BIAS_I, _P_LAMBDA = CONV_WIDTH, CONV_WIDTH + 1, CONV_WIDTH + 2, CONV_WIDTH + 3


def _gate_windows(blk, group):
    width = min(group, _round_up(blk, LANES) + LANES)
    starts = tuple(min(n * blk // LANES * LANES, group - width) for n in range(group // blk))
    return width, starts


def _gelu_tanh(x):
    c = 0.7978845608028654
    return x * (0.5 * (1.0 + jnp.tanh(c * (x + 0.044715 * (x * x * x)))))


def _lru_kernel(x_ref, gb_ref, w_ref, p_ref, o_ref, xs_ref, gs_ref, xc_ref, xb_ref, gates_ref,
                os_ref, h_ref, *, starts, width):
    tt, gw = x_ref.shape
    hist = SUBLANES
    n_tiles = tt // SUBLANES
    n_slabs = gw // LANES

    def phase(ref, slab, first_row):
        return ref[slab, pl.ds(first_row, n_tiles, stride=SUBLANES), :]

    @pl.when(pl.program_id(2) == 0)
    def _():
        h_ref[...] = jnp.zeros_like(h_ref)
        xs_ref[:, 0:hist, :] = jnp.zeros((n_slabs, hist, LANES), F32)

    for s in range(n_slabs):
        lanes = slice(s * LANES, (s + 1) * LANES)
        xs_ref[s, hist:hist + tt, :] = x_ref[:, lanes]
        gs_ref[s] = gb_ref[:, lanes]
    p = p_ref[...]

    for s in range(n_slabs):
        lanes = slice(s * LANES, (s + 1) * LANES)
        taps = {o: phase(xs_ref, s, o) for o in range(hist - (CONV_WIDTH - 1), hist + SUBLANES)}
        for j in range(SUBLANES):
            xc = p[_P_CONV_B:_P_CONV_B + 1, lanes]
            for k in range(CONV_WIDTH):
                xc = xc + p[k:k + 1, lanes] * taps[hist + j - (CONV_WIDTH - 1) + k]
            xc_ref[s, j] = xc
            xb_ref[j * n_tiles:(j + 1) * n_tiles, lanes] = xc.astype(MXU_DTYPE)

    written = set()
    for n, ws in enumerate(starts):
        g = jnp.dot(xb_ref[:, ws:ws + width], w_ref[n], preferred_element_type=F32)
        for c in range(0, width, LANES):
            for src, dst in ((c, ws + c), (width + c, gw + ws + c)):
                if dst in written:
                    gates_ref[:, dst:dst + LANES] += g[:, src:src + LANES]
                else:
                    gates_ref[:, dst:dst + LANES] = g[:, src:src + LANES]
                    written.add(dst)
    assert len(written) == 2 * n_slabs

    tile = lax.broadcasted_iota(jnp.int32, (n_tiles, LANES), 0)
    for s in range(n_slabs):
        lanes = slice(s * LANES, (s + 1) * LANES)
        lanes_i = slice(gw + s * LANES, gw + (s + 1) * LANES)
        bias_r = p[_P_BIAS_R:_P_BIAS_R + 1, lanes]
        bias_i = p[_P_BIAS_I:_P_BIAS_I + 1, lanes]
        neg_lam = -p[_P_LAMBDA:_P_LAMBDA + 1, lanes]
        softplus = jnp.maximum(neg_lam, 0.0) + jnp.log1p(jnp.exp(-jnp.abs(neg_lam)))

        h_loc, a_cum = [], []
        for j in range(SUBLANES):
            rows = slice(j * n_tiles, (j + 1) * n_tiles)
            xc = xc_ref[s, j]
            r = jax.nn.sigmoid(gates_ref[rows, lanes] + bias_r)
            i = jax.nn.sigmoid(gates_ref[rows, lanes_i] + bias_i)
            log_a = (-LRU_C * r) * softplus
            a = jnp.exp(log_a)
            y = 1.0 - a * a
            u = (y * lax.rsqrt(jnp.maximum(y, SQRT_FLOOR))) * (i * xc)
            h_loc.append(u if j == 0 else a * h_loc[-1] + u)
            a_cum.append(a if j == 0 else a * a_cum[-1])

        a_inc, u_inc = a_cum[-1], h_loc[-1]
        d = 1
        while d < n_tiles:
            keep = tile >= d
            a_prev = jnp.where(keep, pltpu.roll(a_inc, d, 0), 1.0)
            u_prev = jnp.where(keep, pltpu.roll(u_inc, d, 0), 0.0)
            u_inc = u_inc + a_inc * u_prev
            a_inc = a_inc * a_prev
            d *= 2
        h_prev = jnp.broadcast_to(h_ref[s], (n_tiles, LANES))
        h_end = u_inc + a_inc * h_prev
        h_in = jnp.where(tile >= 1, pltpu.roll(h_end, 1, 0), h_prev)
        h_ref[s] = h_end[n_tiles - 1:n_tiles, :]

        for j in range(SUBLANES):
            hs = h_loc[j] + a_cum[j] * h_in
            os_ref[s, pl.ds(j, n_tiles, stride=SUBLANES), :] = hs * _gelu_tanh(phase(gs_ref, s, j))
        o_ref[:, lanes] = os_ref[s].astype(o_ref.dtype)

    xs_ref[:, 0:hist, :] = xs_ref[:, tt:tt + hist, :]


def _lru(proj, w_win, params, batch, seq, group, width, starts):
    m = proj.shape[0]
    lru_width = proj.shape[1] // 2
    n_groups = lru_width // group
    n_slabs = group // LANES
    bpg = len(starts)
    tt = _pick_tile(seq, LRU_ROWS, SUBLANES * SUBLANES)
    assert tt % (SUBLANES * SUBLANES) == 0
    tpb = seq // tt
    row = lambda g, b, t: b * tpb + t
    blocks = (2 * _nbytes((tt, group), proj.dtype) + _nbytes((SUBLANES, group), F32)
              + _nbytes((tt, group), MXU_DTYPE))
    scratch = [pltpu.VMEM((n_slabs, tt + SUBLANES, LANES), F32),
               pltpu.VMEM((n_slabs, tt, LANES), F32),
               pltpu.VMEM((n_slabs, SUBLANES, tt // SUBLANES, LANES), F32),
               pltpu.VMEM((tt, group), MXU_DTYPE),
               pltpu.VMEM((tt, 2 * group), F32),
               pltpu.VMEM((n_slabs, tt, LANES), F32),
               pltpu.VMEM((n_slabs, 1, LANES), F32)]
    resident = (_nbytes((bpg, width, 2 * width), MXU_DTYPE) + 4 * _nbytes((tt + SUBLANES, group), F32)
                + _nbytes((tt, group), MXU_DTYPE) + _nbytes((tt, 2 * group), F32)
                + 2 * _nbytes((tt, 2 * width), F32))
    return pl.pallas_call(
        functools.partial(_lru_kernel, starts=starts, width=width),
        grid=(n_groups, batch, tpb),
        in_specs=[
            pl.BlockSpec((tt, group), lambda g, b, t: (row(g, b, t), g)),
            pl.BlockSpec((tt, group), lambda g, b, t: (row(g, b, t), n_groups + g)),
            pl.BlockSpec((bpg, width, 2 * width), lambda g, b, t: (g, 0, 0),
                         pipeline_mode=pl.Buffered(1)),
            pl.BlockSpec((SUBLANES, group), lambda g, b, t: (0, g)),
        ],
        out_specs=pl.BlockSpec((tt, group), lambda g, b, t: (row(g, b, t), g)),
        out_shape=jax.ShapeDtypeStruct((m, lru_width), MXU_DTYPE),
        scratch_shapes=scratch,
        compiler_params=_compiler_params(3, blocks, resident),
        name="rglru",
    )(proj, proj, w_win, params)


def _embed_gate_weights(w_r, w_i, group, width, starts):
    blk = w_r.shape[1]
    bpg = len(starts)
    out = []
    for n in range(w_r.shape[0]):
        off = (n % bpg) * blk - starts[n % bpg]
        pad = ((off, width - blk - off), (off, width - blk - off))
        out.append(jnp.concatenate([jnp.pad(w_r[n], pad), jnp.pad(w_i[n], pad)], axis=1))
    return jnp.stack(out).astype(MXU_DTYPE)


_NORM_BK = 4096
_F32_WEIGHT_TILES = dict(bm=2048, bn=512)


def _mlp(h, hg, ss, w_up, w_down, layer, next_gain):
    z = _matmul(hg, w_up, layer, MXU_DTYPE, epilogue="relu2", row_sumsq=ss)
    if next_gain is None:
        return _matmul(z, w_down, layer, F32, epilogue="residual", res=h), None, None
    return _matmul(z, w_down, layer, F32, epilogue="residual", res=h, next_gain=next_gain,
                   bk=_NORM_BK)


def _retention_layer(h, hg, ss, positions, w_in, w_out, layer, next_gain, batch, seq):
    d = h.shape[1]
    proj = _matmul(hg, w_in, layer, MXU_DTYPE, row_sumsq=ss, **_F32_WEIGHT_TILES)
    cos, sin = _rope_tables(positions, d // RET_HEADS // 2)
    o = _retention(proj, cos, sin, batch, seq, d)
    return _matmul(o, w_out, layer, F32, epilogue="residual", res=h, next_gain=next_gain,
                   bk=_NORM_BK)


def _lru_layer(h, hg, ss, w_in, conv_w, conv_b, w_rgate, b_rgate, w_igate, b_igate, lam, w_out,
               layer, next_gain, batch, seq):
    lru_width = w_out.shape[1]
    blk = lru_width // LRU_BLOCKS
    group = blk * LANES // math.gcd(blk, LANES)
    width, starts = _gate_windows(blk, group)
    w_win = _embed_gate_weights(w_rgate, w_igate, group, width, starts)
    params = jnp.concatenate(
        [conv_w] + [v.reshape(1, lru_width) for v in (conv_b, b_rgate, b_igate, lam)],
        axis=0).astype(F32)
    proj = _matmul(hg, w_in, layer, F32, row_sumsq=ss, x_double_buffered=False, **_F32_WEIGHT_TILES)
    mixed = _lru(proj, w_win, params, batch, seq, group, width, starts)
    return _matmul(mixed, w_out, layer, F32, epilogue="residual", res=h, next_gain=next_gain,
                   bk=_NORM_BK)


def kernel(x, positions, norm_mix_g, norm_mlp_g, final_norm_g, ret_w_in, ret_w_out,
           lru_w_in, lru_conv_w, lru_conv_b, lru_w_rgate, lru_b_rgate, lru_w_igate,
           lru_b_igate, lru_lambda, lru_w_out, mlp_w_up, mlp_w_down):
    batch, seq, d = x.shape
    h = x.reshape(batch * seq, d)
    depth = norm_mix_g.shape[0]
    n_mixers = 2
    cast = lambda w: w.astype(MXU_DTYPE)
    ret_w_out, lru_w_out, mlp_w_up, mlp_w_down = map(
        cast, (ret_w_out, lru_w_out, mlp_w_up, mlp_w_down))
    hg, ss = _norm_operand(h, norm_mix_g[0])
    for i in range(depth):
        j = i // n_mixers
        if i % n_mixers == 0:
            h, hg, ss = _retention_layer(h, hg, ss, positions, ret_w_in, ret_w_out, j,
                                         norm_mlp_g[i], batch, seq)
        else:
            h, hg, ss = _lru_layer(h, hg, ss, lru_w_in, lru_conv_w[j], lru_conv_b[j],
                                   lru_w_rgate[j], lru_b_rgate[j], lru_w_igate[j],
                                   lru_b_igate[j], lru_lambda[j], lru_w_out, j,
                                   norm_mlp_g[i], batch, seq)
        next_gain = norm_mix_g[i + 1] if i + 1 < depth else None
        h, hg, ss = _mlp(h, hg, ss, mlp_w_up, mlp_w_down, i, next_gain)
    return _rmsnorm(h, final_norm_g, x.dtype).reshape(batch, seq, d)
```
